```python
import jax, jax.numpy as jnp
from jax import lax
import numpy as np

D_MODEL = 1024
BATCH = 4
SEQ = 8192
DEPTH = 2

GRID_W = 64
CTX_LEN = 256
HEAD_DIM = 64
N_HEADS_A = 8
N_KV_A = 2
N_HEADS_B = 8
N_KV_B = 2
BRANCH_WIDTH = 512
CONV_K = 3
WINDOW = 128
Q_BLOCK = 128
N_BRANCH = 3
N_EXPERTS = 16
EXPERT_HIDDEN = 1024
CAPACITY_FACTOR = 2
ROPE_BASE = 10000.0
EPS = 1e-6
NEG_INF = -1e30

KV_A = N_KV_A * HEAD_DIM
KV_B = N_KV_B * HEAD_DIM
KV_END = 2 * KV_A + 2 * KV_B
OFF_QA = KV_END
OFF_QB = OFF_QA + N_HEADS_A * HEAD_DIM
OFF_CONV = OFF_QB + N_HEADS_B * HEAD_DIM
OFF_GATE = OFF_CONV + 3 * BRANCH_WIDTH
PROJ_WIDTH = OFF_GATE + N_BRANCH * D_MODEL

kernel_name = "hybrid_parallel_gqa_window_shortconv_ecmoe_dit"


def rmsnorm(x, g):
    x32 = x.astype(jnp.float32)
    y = x32 * lax.rsqrt(jnp.mean(x32 * x32, axis=-1, keepdims=True) + EPS)
    return (y * g.astype(jnp.float32)).astype(x.dtype)


def modulate(h, shift, scale):
    return h * (1 + scale) + shift


def rope_tables(n):
    rows = n // GRID_W
    row = jnp.repeat(jnp.arange(rows, dtype=jnp.float32), GRID_W)
    col = jnp.tile(jnp.arange(GRID_W, dtype=jnp.float32), rows)
    n_freq = HEAD_DIM // 4
    inv = ROPE_BASE ** (-jnp.arange(n_freq, dtype=jnp.float32) / n_freq)
    ang = jnp.concatenate([row[:, None] * inv, col[:, None] * inv], axis=-1)
    return jnp.cos(ang), jnp.sin(ang)


def apply_rope(x, cos, sin):
    half = HEAD_DIM // 2
    c = cos[None, :, None, :].astype(x.dtype)
    s = sin[None, :, None, :].astype(x.dtype)
    x1, x2 = x[..., :half], x[..., half:]
    return jnp.concatenate([x1 * c - x2 * s, x2 * c + x1 * s], axis=-1)


def q_heads(p_q, qg, n_heads, n_kv, rope):
    B, n, _ = p_q.shape
    q = rmsnorm(p_q.reshape(B, n, n_heads, HEAD_DIM), qg)
    if rope is not None:
        q = apply_rope(q, *rope)
    return q.reshape(B, n, n_kv, n_heads // n_kv, HEAD_DIM)


def kv_heads(p_kv, kg_a, kg_b, rope):
    B, n, _ = p_kv.shape
    ka, va, kb, vb = jnp.split(p_kv, [KV_A, 2 * KV_A, 2 * KV_A + KV_B], axis=-1)
    ka = rmsnorm(ka.reshape(B, n, N_KV_A, HEAD_DIM), kg_a)
    kb = rmsnorm(kb.reshape(B, n, N_KV_B, HEAD_DIM), kg_b)
    va = va.reshape(B, n, N_KV_A, HEAD_DIM)
    vb = vb.reshape(B, n, N_KV_B, HEAD_DIM)
    if rope is not None:
        ka = apply_rope(ka, *rope)
        kb = apply_rope(kb, *rope)
    return ka, va, kb, vb


def attend(q, k, v, mask, sink):
    s = jnp.einsum('bqkgd,bskd->bkgqs', q, k, preferred_element_type=jnp.float32) * (HEAD_DIM ** -0.5)
    if mask is not None:
        s = jnp.where(mask, s, NEG_INF)
    if sink is not None:
        sk = jnp.broadcast_to(sink.astype(jnp.float32)[None, :, :, None, None], s.shape[:-1] + (1,))
        p = jax.nn.softmax(jnp.concatenate([s, sk], axis=-1), axis=-1)[..., :-1]
    else:
        p = jax.nn.softmax(s, axis=-1)
    return jnp.einsum('bkgqs,bskd->bqkgd', p.astype(v.dtype), v)


def to_blocks(q):
    B, n, KV, G, d = q.shape
    return q.reshape(B, n // Q_BLOCK, Q_BLOCK, KV, G, d).transpose(1, 0, 2, 3, 4, 5)


def from_blocks(o, n):
    nblk, B, qb, KV, G, d = o.shape
    return o.transpose(1, 0, 2, 3, 4, 5).reshape(B, n, KV * G * d)


def global_attention(q, k_lat, v_lat, k_ctx, v_ctx):
    k = jnp.concatenate([k_lat, k_ctx], axis=1)
    v = jnp.concatenate([v_lat, v_ctx], axis=1)
    o = lax.map(lambda qb: attend(qb, k, v, None, None), to_blocks(q))
    return from_blocks(o, q.shape[1])


def window_attention(q, k_lat, v_lat, k_ctx, v_ctx, sink):
    n = q.shape[1]
    nblk = n // Q_BLOCK
    pad = ((0, 0), (WINDOW, WINDOW), (0, 0), (0, 0))
    k_pad = jnp.pad(k_lat, pad)
    v_pad = jnp.pad(v_lat, pad)
    span = Q_BLOCK + 2 * WINDOW
    ctx_mask = jnp.ones((Q_BLOCK, k_ctx.shape[1]), dtype=bool)

    def block(args):
        i, qb = args
        start = i * Q_BLOCK
        kb = lax.dynamic_slice_in_dim(k_pad, start, span, axis=1)
        vb = lax.dynamic_slice_in_dim(v_pad, start, span, axis=1)
        qpos = start + jnp.arange(Q_BLOCK)
        kpos = start - WINDOW + jnp.arange(span)
        local = (jnp.abs(qpos[:, None] - kpos[None, :]) <= WINDOW) & (kpos >= 0)[None, :] & (kpos < n)[None, :]
        mask = jnp.concatenate([local, ctx_mask], axis=1)
        return attend(qb, jnp.concatenate([kb, k_ctx], axis=1), jnp.concatenate([vb, v_ctx], axis=1), mask, sink)

    o = lax.map(block, (jnp.arange(nblk), to_blocks(q)))
    return from_blocks(o, n)


def short_conv(p_conv, w):
    b_gate, c_gate, x_in = jnp.split(p_conv, 3, axis=-1)
    u = c_gate * x_in
    n = u.shape[1]
    up = jnp.pad(u, ((0, 0), (CONV_K // 2, CONV_K // 2), (0, 0)))
    conv = sum(w[j] * up[:, j:j + n] for j in range(CONV_K))
    return b_gate * conv


def merge_branches(o_a, o_b, o_c, gate_logits, w_branch, w_out):
    g_a, g_b, g_c = jnp.split(jax.nn.sigmoid(gate_logits), 3, axis=-1)
    merged = g_a * (o_a @ w_branch[0]) + g_b * (o_b @ w_branch[1]) + g_c * (o_c @ w_branch[2])
    return merged @ w_out


def latent_mixer(p, ctx_kv, rope, qg_a, kg_a, qg_b, kg_b, sink, conv_w, w_branch, w_out):
    ka, va, kb, vb = kv_heads(p[..., :KV_END], kg_a, kg_b, rope)
    ka_c, va_c, kb_c, vb_c = ctx_kv
    qa = q_heads(p[..., OFF_QA:OFF_QB], qg_a, N_HEADS_A, N_KV_A, rope)
    qb = q_heads(p[..., OFF_QB:OFF_CONV], qg_b, N_HEADS_B, N_KV_B, rope)
    o_a = global_attention(qa, ka, va, ka_c, va_c)
    o_b = window_attention(qb, kb, vb, kb_c, vb_c, sink.reshape(N_KV_B, N_HEADS_B // N_KV_B))
    o_c = short_conv(p[..., OFF_CONV:OFF_GATE], conv_w)
    return merge_branches(o_a, o_b, o_c, p[..., OFF_GATE:], w_branch, w_out)


def context_mixer(pc, ctx_kv, qg_a, qg_b, sink, conv_w, w_branch, w_out):
    B, L, _ = pc.shape
    ka_c, va_c, kb_c, vb_c = ctx_kv
    qa = q_heads(pc[..., OFF_QA:OFF_QB], qg_a, N_HEADS_A, N_KV_A, None)
    qb = q_heads(pc[..., OFF_QB:OFF_CONV], qg_b, N_HEADS_B, N_KV_B, None)
    o_a = attend(qa, ka_c, va_c, None, None).reshape(B, L, -1)
    o_b = attend(qb, kb_c, vb_c, None, sink.reshape(N_KV_B, N_HEADS_B // N_KV_B)).reshape(B, L, -1)
    o_c = short_conv(pc[..., OFF_CONV:OFF_GATE], conv_w)
    return merge_branches(o_a, o_b, o_c, pc[..., OFF_GATE:], w_branch, w_out)


def expert_choice_ffn(h, w_router, w_g, w_u, w_d):
    B, n, D = h.shape
    cap = CAPACITY_FACTOR * n // N_EXPERTS
    logits = jnp.einsum('bnd,de->ben', h, w_router, preferred_element_type=jnp.float32)
    aff = jax.nn.softmax(logits, axis=1)
    gate, idx = lax.top_k(aff, cap)
    bi = jnp.arange(B)[:, None, None]
    xs = h[bi, idx]
    a = jnp.einsum('becd,edf->becf', xs, w_g)
    u = jnp.einsum('becd,edf->becf', xs, w_u)
    y = jnp.einsum('becf,efd->becd', jax.nn.silu(a) * u, w_d)
    return jnp.zeros_like(h).at[bi, idx].add(y * gate[..., None].astype(y.dtype))


def setup_inputs(seed: int = 0) -> dict:
    key = jax.random.key(seed)
    ks = jax.random.split(key, 22)
    D = D_MODEL

    def nrm(k, shape, scale):
        return jax.random.normal(k, shape, jnp.float32) * scale

    return {
        "x": nrm(ks[0], (BATCH, SEQ, D), 1.0),
        "c": nrm(ks[1], (BATCH, D), 1.0),
        "ctx": nrm(ks[2], (BATCH, CTX_LEN, D), 1.0),
        "c_ctx": nrm(ks[3], (D,), 1.0),
        "w_ada": nrm(ks[4], (DEPTH, D, 6 * D), 0.5 * D ** -0.5),
        "b_ada": nrm(ks[5], (DEPTH, 6 * D), 0.02),
        "g_mix": 1.0 + nrm(ks[6], (DEPTH, D), 0.02),
        "g_ffn": 1.0 + nrm(ks[7], (DEPTH, D), 0.02),
        "w_in": nrm(ks[8], (DEPTH, D, PROJ_WIDTH), D ** -0.5),
        "qg_a": 1.0 + nrm(ks[9], (DEPTH, HEAD_DIM), 0.02),
        "kg_a": 1.0 + nrm(ks[10], (DEPTH, HEAD_DIM), 0.02),
        "qg_b": 1.0 + nrm(ks[11], (DEPTH, HEAD_DIM), 0.02),
        "kg_b": 1.0 + nrm(ks[12], (DEPTH, HEAD_DIM), 0.02),
        "sink_b": nrm(ks[13], (DEPTH, N_HEADS_B), 0.5),
        "conv_w": nrm(ks[14], (DEPTH, CONV_K, BRANCH_WIDTH), CONV_K ** -0.5),
        "w_branch": nrm(ks[15], (DEPTH, N_BRANCH, BRANCH_WIDTH, D), BRANCH_WIDTH ** -0.5),
        "w_out": nrm(ks[16], (DEPTH, D, D), D ** -0.5),
        "w_router": nrm(ks[17], (DEPTH, D, N_EXPERTS), D ** -0.5),
        "w_e_gate": nrm(ks[18], (DEPTH, N_EXPERTS, D, EXPERT_HIDDEN), D ** -0.5),
        "w_e_up": nrm(ks[19], (DEPTH, N_EXPERTS, D, EXPERT_HIDDEN), D ** -0.5),
        "w_e_down": nrm(ks[20], (DEPTH, N_EXPERTS, EXPERT_HIDDEN, D), EXPERT_HIDDEN ** -0.5),
    }


def reference(x, c, ctx, c_ctx, w_ada, b_ada, g_mix, g_ffn, w_in, qg_a, kg_a, qg_b, kg_b, sink_b,
              conv_w, w_branch, w_out, w_router, w_e_gate, w_e_up, w_e_down):
    n = x.shape[1]
    rope = rope_tables(n)
    silu_c = jax.nn.silu(c)
    silu_cc = jax.nn.silu(c_ctx)
    xc = ctx
    for l in range(DEPTH):
        last = l == DEPTH - 1
        mod = silu_c @ w_ada[l] + b_ada[l]
        sh1, sc1, gt1, sh2, sc2, gt2 = [m[:, None, :] for m in jnp.split(mod, 6, axis=-1)]
        modc = silu_cc @ w_ada[l] + b_ada[l]
        shc1, scc1, gtc1, shc2, scc2, gtc2 = jnp.split(modc, 6)

        hc = modulate(rmsnorm(xc, g_mix[l]), shc1, scc1)
        pc = hc @ (w_in[l][:, :KV_END] if last else w_in[l])
        ctx_kv = kv_heads(pc[..., :KV_END], kg_a[l], kg_b[l], None)

        h = modulate(rmsnorm(x, g_mix[l]), sh1, sc1)
        p = h @ w_in[l]
        x = x + gt1 * latent_mixer(p, ctx_kv, rope, qg_a[l], kg_a[l], qg_b[l], kg_b[l], sink_b[l],
                                   conv_w[l], w_branch[l], w_out[l])
        h2 = modulate(rmsnorm(x, g_ffn[l]), sh2, sc2)
        x = x + gt2 * expert_choice_ffn(h2, w_router[l], w_e_gate[l], w_e_up[l], w_e_down[l])

        if not last:
            xc = xc + gtc1 * context_mixer(pc, ctx_kv, qg_a[l], qg_b[l], sink_b[l], conv_w[l],
                                           w_branch[l], w_out[l])
            hc2 = modulate(rmsnorm(xc, g_ffn[l]), shc2, scc2)
            xc = xc + gtc2 * expert_choice_ffn(hc2, w_router[l], w_e_gate[l], w_e_up[l], w_e_down[l])
    return x
```

```python
import functools

import jax
import jax.numpy as jnp
from jax import lax
from jax.experimental import pallas as pl
from jax.experimental.pallas import tpu as pltpu

F32 = jnp.float32
BF16 = jnp.bfloat16
I32 = jnp.int32

HEAD_DIM = 64
N_HEADS = 8
N_KV = 2
GROUP = N_HEADS // N_KV
BRANCH = 512
N_EXPERTS = 16
CAPACITY_FACTOR = 2
GRID_W = 64
WINDOW = 128
ROPE_BASE = 10000.0
EPS = 1e-6
NEG_INF = -1e30

KV_W = N_KV * HEAD_DIM
OFF_QA = 4 * KV_W
OFF_QB = OFF_QA + N_HEADS * HEAD_DIM
OFF_CONV = OFF_QB + N_HEADS * HEAD_DIM
OFF_GATE = OFF_CONV + 3 * BRANCH

LANES = 128
MXU_DIM = 256
BF16_SUBLANES = 16
VMEM_LIMIT = 56 * 1024 * 1024


def _cparams(sem):
    return pltpu.CompilerParams(dimension_semantics=sem, vmem_limit_bytes=VMEM_LIMIT)


def _ada_kernel(s_ref, w_ref, b_ref, o_ref):
    s = s_ref[...]
    s = s * jax.nn.sigmoid(s)
    o_ref[0] = jnp.dot(s, w_ref[0], preferred_element_type=F32,
                       precision=lax.Precision.HIGHEST) + b_ref[0]


def _ada(s, w_ada, b_ada):
    depth, d, d6 = w_ada.shape
    rows = s.shape[0]
    tn = 1536
    return pl.pallas_call(
        _ada_kernel,
        grid=(depth, d6 // tn),
        in_specs=[
            pl.BlockSpec((rows, d), lambda l, j: (0, 0)),
            pl.BlockSpec((1, d, tn), lambda l, j: (l, 0, j)),
            pl.BlockSpec((1, 1, tn), lambda l, j: (l, 0, j)),
        ],
        out_specs=pl.BlockSpec((1, rows, tn), lambda l, j: (l, 0, j)),
        out_shape=jax.ShapeDtypeStruct((depth, rows, d6), F32),
        compiler_params=_cparams(("arbitrary", "arbitrary")),
        name="ada",
    )(s, w_ada, b_ada.reshape(depth, 1, d6))


def _inproj_kernel(x_ref, sh_ref, sc_ref, g_ref, w_ref, gq_ref, gk_ref, cs_ref, sn_ref,
                   qa_ref, qb_ref, kta_ref, va_ref, ktb_ref, vb_ref, bg_ref, u_ref, gt_ref):
    x = x_ref[0]
    tm = x.shape[0]
    ms = jnp.mean(x * x, axis=-1, keepdims=True)
    h = (x * lax.rsqrt(ms + EPS)) * g_ref[...]
    h = h * (1.0 + sc_ref[0]) + sh_ref[0]
    hb = h.astype(BF16)

    def proj(a, b):
        return jnp.dot(hb, w_ref[:, a:b], preferred_element_type=F32)

    slab = 2 * LANES
    r = lax.broadcasted_iota(I32, (slab, slab), 0) // HEAD_DIM
    c = lax.broadcasted_iota(I32, (slab, slab), 1) // HEAD_DIM
    head_mean = jnp.where(r == c, 1.0 / HEAD_DIM, 0.0).astype(BF16)
    cs = jnp.concatenate([cs_ref[...], cs_ref[...]], axis=1)
    sn = jnp.concatenate([sn_ref[...], sn_ref[...]], axis=1)
    lane = lax.broadcasted_iota(I32, (tm, slab), 1)
    first_half = (lane % HEAD_DIM) < (HEAD_DIM // 2)

    def head_norm_rope(p, gain):
        msq = jnp.dot((p * p).astype(BF16), head_mean, preferred_element_type=F32)
        y = p * lax.rsqrt(msq + EPS) * gain
        swapped = jnp.where(first_half, pltpu.roll(y, slab - HEAD_DIM // 2, 1), pltpu.roll(y, HEAD_DIM // 2, 1))
        return y * cs + swapped * sn

    pkv = proj(0, OFF_QA)
    k = jnp.concatenate([pkv[:, 0:KV_W], pkv[:, 2 * KV_W:3 * KV_W]], axis=1)
    k = head_norm_rope(k, gk_ref[...])
    kt = jnp.transpose(k).astype(BF16)
    kta_ref[0, 0] = kt[0:HEAD_DIM]
    kta_ref[0, 1] = kt[HEAD_DIM:2 * HEAD_DIM]
    ktb_ref[0, 0] = kt[2 * HEAD_DIM:3 * HEAD_DIM]
    ktb_ref[0, 1] = kt[3 * HEAD_DIM:4 * HEAD_DIM]

    lane128 = lax.broadcasted_iota(I32, (tm, LANES), 1)
    ones_col = jnp.where(lane128 == HEAD_DIM, 1.0, 0.0)

    def v_heads(v, ref):
        ref[0, 0] = jnp.where(lane128 < HEAD_DIM, v, ones_col).astype(BF16)
        ref[0, 1] = jnp.where(lane128 < HEAD_DIM, pltpu.roll(v, HEAD_DIM, 1), ones_col).astype(BF16)

    v_heads(pkv[:, KV_W:2 * KV_W], va_ref)
    v_heads(pkv[:, 3 * KV_W:4 * KV_W], vb_ref)

    pq = proj(OFF_QA, OFF_CONV)
    for s in range(4):
        q = head_norm_rope(pq[:, s * slab:(s + 1) * slab], gq_ref[:, s * slab:(s + 1) * slab])
        ref = qa_ref if s < 2 else qb_ref
        ref[0, :, (s % 2) * slab:(s % 2 + 1) * slab] = q.astype(BF16)

    pc = proj(OFF_CONV, OFF_GATE)
    bg_ref[0] = pc[:, 0:BRANCH].astype(BF16)
    u_ref[0] = (pc[:, BRANCH:2 * BRANCH] * pc[:, 2 * BRANCH:3 * BRANCH]).astype(BF16)

    d = x.shape[1]
    for j in range(3):
        pg = proj(OFF_GATE + j * d, OFF_GATE + (j + 1) * d)
        gt_ref[0, :, j * d:(j + 1) * d] = jax.nn.sigmoid(pg).astype(BF16)


def _inproj(x, sh, sc, g, w_bf, gq, gk, cs, sn, tm):
    b, n, d = x.shape
    pw = w_bf.shape[1]
    nq = N_HEADS * HEAD_DIM
    bmap = lambda bi, i: (bi, i, 0)
    out_shape = [
        jax.ShapeDtypeStruct((b, n, nq), BF16), jax.ShapeDtypeStruct((b, n, nq), BF16),
        jax.ShapeDtypeStruct((b, N_KV, HEAD_DIM, n), BF16), jax.ShapeDtypeStruct((b, N_KV, n, LANES), BF16),
        jax.ShapeDtypeStruct((b, N_KV, HEAD_DIM, n), BF16), jax.ShapeDtypeStruct((b, N_KV, n, LANES), BF16),
        jax.ShapeDtypeStruct((b, n, BRANCH), BF16), jax.ShapeDtypeStruct((b, n, BRANCH), BF16),
        jax.ShapeDtypeStruct((b, n, 3 * d), BF16),
    ]
    kt_spec = pl.BlockSpec((1, N_KV, HEAD_DIM, tm), lambda bi, i: (bi, 0, 0, i))
    v_spec = pl.BlockSpec((1, N_KV, tm, LANES), lambda bi, i: (bi, 0, i, 0))
    out_specs = [
        pl.BlockSpec((1, tm, nq), bmap), pl.BlockSpec((1, tm, nq), bmap),
        kt_spec, v_spec, kt_spec, v_spec,
        pl.BlockSpec((1, tm, BRANCH), bmap), pl.BlockSpec((1, tm, BRANCH), bmap),
        pl.BlockSpec((1, tm, 3 * d), bmap),
    ]
    in_specs = [
        pl.BlockSpec((1, tm, d), bmap),
        pl.BlockSpec((1, 1, d), lambda bi, i: (bi, 0, 0)),
        pl.BlockSpec((1, 1, d), lambda bi, i: (bi, 0, 0)),
        pl.BlockSpec((1, d), lambda bi, i: (0, 0)),
        pl.BlockSpec((d, pw), lambda bi, i: (0, 0), pipeline_mode=pl.Buffered(1)),
        pl.BlockSpec((1, 2 * nq), lambda bi, i: (0, 0)),
        pl.BlockSpec((1, 2 * LANES), lambda bi, i: (0, 0)),
        pl.BlockSpec((tm, LANES), lambda bi, i: (i, 0)),
        pl.BlockSpec((tm, LANES), lambda bi, i: (i, 0)),
    ]
    return pl.pallas_call(
        _inproj_kernel,
        grid=(b, n // tm),
        in_specs=in_specs,
        out_specs=out_specs,
        out_shape=out_shape,
        compiler_params=_cparams(("parallel", "parallel")),
        name="inproj",
    )(x, sh, sc, g, w_bf, gq, gk, cs, sn)


def _attn_kernel(*refs, mode, has_sink, tq, tk, n_lat):
    it = iter(refs)
    q_ref = next(it)
    if mode != "none":
        ktl_ref = next(it)
        vl_ref = next(it)
    ktc_ref = next(it)
    vc_ref = next(it)
    sink_ref = next(it) if has_sink else None
    o_ref = next(it)
    qs_scr = next(it)
    m_scr = next(it)
    acc_scr = next(it)

    i = pl.program_id(2)
    rows = GROUP * tq
    q = q_ref[0]
    for g in range(GROUP):
        qs_scr[g * tq:(g + 1) * tq, :] = q[:, g * HEAD_DIM:(g + 1) * HEAD_DIM]
    if has_sink:
        m_scr[...] = sink_ref[0]
    else:
        m_scr[...] = jnp.full((rows, 1), NEG_INF, F32)
    acc_scr[...] = jnp.zeros((rows, LANES), F32)

    def step(kt, v, mask):
        s = jnp.dot(qs_scr[...], kt, preferred_element_type=F32)
        if mask is not None:
            s = jnp.where(mask, s, NEG_INF)
        m_prev = m_scr[...]
        m_new = jnp.maximum(m_prev, jnp.max(s, axis=1, keepdims=True))
        alpha = jnp.exp(m_prev - m_new)
        p = jnp.exp(s - m_new).astype(BF16)
        acc_scr[...] = alpha * acc_scr[...] + jnp.dot(p, v, preferred_element_type=F32)
        m_scr[...] = m_new

    if mode == "global":
        def body(j, carry):
            k0 = pl.multiple_of(j * tk, tk)
            step(ktl_ref[0, 0, :, pl.ds(k0, tk)], vl_ref[0, 0, pl.ds(k0, tk), :], None)
            return carry
        lax.fori_loop(0, n_lat // tk, body, 0)
    elif mode == "window":
        span = tq + 2 * WINDOW
        start = jnp.clip(i * tq - WINDOW, 0, n_lat - span)
        start = pl.multiple_of(start, LANES)
        qpos = i * tq + lax.broadcasted_iota(I32, (rows, span), 0) % tq
        kpos = start + lax.broadcasted_iota(I32, (rows, span), 1)
        mask = jnp.abs(qpos - kpos) <= WINDOW
        step(ktl_ref[0, 0, :, pl.ds(start, span)], vl_ref[0, 0, pl.ds(start, span), :], mask)
    step(ktc_ref[0, 0], vc_ref[0, 0], None)

    acc = acc_scr[...]
    denom = acc[:, HEAD_DIM:HEAD_DIM + 1]
    if has_sink:
        denom = denom + jnp.exp(sink_ref[0] - m_scr[...])
    o = acc[:, 0:HEAD_DIM] / denom
    o_ref[0] = jnp.concatenate([o[g * tq:(g + 1) * tq] for g in range(GROUP)], axis=1).astype(BF16)


def _attention(q, kt_lat, v_lat, kt_ctx, v_ctx, sink, mode, tq, tk):
    b, nq, _ = q.shape
    l_ctx = kt_ctx.shape[3]
    n_lat = kt_lat.shape[3] if mode != "none" else 0
    gw = GROUP * HEAD_DIM
    rows = GROUP * tq
    kvmap = lambda bi, kv, i: (bi, kv, 0, 0)
    in_specs = [pl.BlockSpec((1, tq, gw), lambda bi, kv, i: (bi, i, kv))]
    args = [q]
    if mode != "none":
        in_specs += [pl.BlockSpec((1, 1, HEAD_DIM, n_lat), kvmap), pl.BlockSpec((1, 1, n_lat, LANES), kvmap)]
        args += [kt_lat, v_lat]
    in_specs += [pl.BlockSpec((1, 1, HEAD_DIM, l_ctx), kvmap), pl.BlockSpec((1, 1, l_ctx, LANES), kvmap)]
    args += [kt_ctx, v_ctx]
    if sink is not None:
        in_specs.append(pl.BlockSpec((1, rows, 1), lambda bi, kv, i: (kv, 0, 0)))
        args.append(sink)
    kern = functools.partial(_attn_kernel, mode=mode, has_sink=sink is not None, tq=tq, tk=tk, n_lat=n_lat)
    return pl.pallas_call(
        kern,
        grid=(b, N_KV, nq // tq),
        in_specs=in_specs,
        out_specs=pl.BlockSpec((1, tq, gw), lambda bi, kv, i: (bi, i, kv)),
        out_shape=jax.ShapeDtypeStruct((b, nq, N_HEADS * HEAD_DIM), BF16),
        scratch_shapes=[pltpu.VMEM((rows, HEAD_DIM), BF16), pltpu.VMEM((rows, 1), F32),
                        pltpu.VMEM((rows, LANES), F32)],
        compiler_params=_cparams(("parallel", "parallel", "parallel")),
        name="attn_" + mode + ("_sink" if sink is not None else ""),
    )(*args)


def _merge_kernel(oa_ref, ob_ref, bg_ref, u_ref, up_ref, un_ref, gts_ref, x_ref, gt1_ref, cw_ref, wb_ref, wo_ref,
                  gf_ref, sh2_ref, sc2_ref, wr_ref, x1_ref, h2_ref, aff_ref, *, nt):
    i = pl.program_id(1)
    tm, d = x_ref.shape[1], x_ref.shape[2]
    u = u_ref[0].astype(F32)
    prev_row = jnp.where(i > 0, up_ref[0][BF16_SUBLANES - 1:BF16_SUBLANES, :].astype(F32), 0.0)
    next_row = jnp.where(i < nt - 1, un_ref[0][0:1, :].astype(F32), 0.0)
    row = lax.broadcasted_iota(I32, (tm, BRANCH), 0)
    um1 = jnp.where(row == 0, prev_row, pltpu.roll(u, 1, 0))
    up1 = jnp.where(row == tm - 1, next_row, pltpu.roll(u, tm - 1, 0))
    cw = cw_ref[...]
    conv = cw[0:1] * um1 + cw[1:2] * u + cw[2:3] * up1
    oc = (bg_ref[0].astype(F32) * conv).astype(BF16)

    merged = gts_ref[0, :, 0:d].astype(F32) * jnp.dot(oa_ref[0], wb_ref[0], preferred_element_type=F32)
    merged += gts_ref[0, :, d:2 * d].astype(F32) * jnp.dot(ob_ref[0], wb_ref[1], preferred_element_type=F32)
    merged += gts_ref[0, :, 2 * d:3 * d].astype(F32) * jnp.dot(oc, wb_ref[2], preferred_element_type=F32)
    y = jnp.dot(merged.astype(BF16), wo_ref[...], preferred_element_type=F32)
    x1 = x_ref[0] + gt1_ref[0] * y
    x1_ref[0] = x1

    ms = jnp.mean(x1 * x1, axis=-1, keepdims=True)
    h2 = (x1 * lax.rsqrt(ms + EPS)) * gf_ref[...]
    h2 = h2 * (1.0 + sc2_ref[0]) + sh2_ref[0]
    h_hi = h2.astype(BF16)
    h2_ref[0] = h_hi

    h_lo = (h2 - h_hi.astype(F32)).astype(BF16)
    wr = wr_ref[...]
    wr_hi = wr.astype(BF16)
    wr_lo = (wr - wr_hi.astype(F32)).astype(BF16)
    dn = (((1,), (1,)), ((), ()))
    lg = lax.dot_general(wr_hi, h_hi, dn, preferred_element_type=F32)
    lg += lax.dot_general(wr_hi, h_lo, dn, preferred_element_type=F32)
    lg += lax.dot_general(wr_lo, h_hi, dn, preferred_element_type=F32)
    ex = jnp.exp(lg - jnp.max(lg, axis=0, keepdims=True))
    aff_ref[0] = ex / jnp.sum(ex, axis=0, keepdims=True)


def _merge(oa, ob, bg, u, gts, x, gt1, cw, wb_bf, wo_bf, gf, sh2, sc2, wr_t, tm):
    b, n, d = x.shape
    nt = n // tm
    e = wr_t.shape[0]
    bmap = lambda bi, i: (bi, i, 0)
    vec = lambda bi, i: (bi, 0, 0)
    halo = BF16_SUBLANES
    per = tm // halo
    last = n // halo - 1
    in_specs = [
        pl.BlockSpec((1, tm, BRANCH), bmap), pl.BlockSpec((1, tm, BRANCH), bmap),
        pl.BlockSpec((1, tm, BRANCH), bmap), pl.BlockSpec((1, tm, BRANCH), bmap),
        pl.BlockSpec((1, halo, BRANCH), lambda bi, i: (bi, jnp.maximum(i * per - 1, 0), 0)),
        pl.BlockSpec((1, halo, BRANCH), lambda bi, i: (bi, jnp.minimum((i + 1) * per, last), 0)),
        pl.BlockSpec((1, tm, 3 * d), bmap),
        pl.BlockSpec((1, tm, d), bmap),
        pl.BlockSpec((1, 1, d), vec),
        pl.BlockSpec((3, BRANCH), lambda bi, i: (0, 0)),
        pl.BlockSpec((3, BRANCH, d), lambda bi, i: (0, 0, 0)),
        pl.BlockSpec((d, d), lambda bi, i: (0, 0)),
        pl.BlockSpec((1, d), lambda bi, i: (0, 0)),
        pl.BlockSpec((1, 1, d), vec), pl.BlockSpec((1, 1, d), vec),
        pl.BlockSpec((e, d), lambda bi, i: (0, 0)),
    ]
    out_specs = [pl.BlockSpec((1, tm, d), bmap), pl.BlockSpec((1, tm, d), bmap),
                 pl.BlockSpec((1, e, tm), lambda bi, i: (bi, 0, i))]
    out_shape = [jax.ShapeDtypeStruct((b, n, d), F32), jax.ShapeDtypeStruct((b, n, d), BF16),
                 jax.ShapeDtypeStruct((b, e, n), F32)]
    return pl.pallas_call(
        functools.partial(_merge_kernel, nt=nt),
        grid=(b, nt),
        in_specs=in_specs,
        out_specs=out_specs,
        out_shape=out_shape,
        compiler_params=_cparams(("parallel", "parallel")),
        name="merge",
    )(oa, ob, bg, u, u, u, gts, x, gt1, cw, wb_bf, wo_bf, gf, sh2, sc2, wr_t)


META_P, META_LO, META_HI = 0, 64, 96


def _cumsum_lanes(x, chunk):
    r = lax.broadcasted_iota(I32, (chunk, chunk), 0)
    c = lax.broadcasted_iota(I32, (chunk, chunk), 1)
    upper = jnp.where(r <= c, 1.0, 0.0).astype(BF16)
    carry = jnp.zeros((x.shape[0], 1), F32)
    outs, starts = [], []
    for j in range(x.shape[1] // chunk):
        starts.append(carry)
        y = jnp.dot(x[:, j * chunk:(j + 1) * chunk].astype(BF16), upper, preferred_element_type=F32) + carry
        outs.append(y)
        carry = y[:, chunk - 1:chunk]
    return jnp.concatenate(outs, axis=1), starts


def _route_kernel(aff_ref, posm_ref, post_ref, meta_ref, *, cap, rt, chunk):
    a = aff_ref[0]
    e, n = a.shape
    bits = pltpu.bitcast(a, I32)

    def search(it, cur):
        cand = cur | jnp.left_shift(jnp.int32(1), 30 - it)
        cnt = jnp.sum(jnp.where(bits >= cand, 1.0, 0.0), axis=1, keepdims=True)
        return jnp.where(cnt >= cap, cand, cur)

    tau = lax.fori_loop(0, 31, search, jnp.zeros((e, 1), I32))
    gt = bits > tau
    eq = bits == tau
    need = cap - jnp.sum(jnp.where(gt, 1.0, 0.0), axis=1, keepdims=True)
    eqf = jnp.where(eq, 1.0, 0.0)
    ceq, _ = _cumsum_lanes(eqf, chunk)
    sel = gt | (eq & ((ceq - eqf) < need))
    sf = jnp.where(sel, 1.0, 0.0)
    cin, starts = _cumsum_lanes(sf, chunk)
    posm = jnp.where(sel, cin - sf, -1.0)
    posm_ref[0] = posm
    pad = jnp.full((LANES - e, n), -1.0, F32)
    post_ref[0] = jnp.transpose(jnp.concatenate([posm, pad], axis=0))

    lane = lax.broadcasted_iota(I32, (e, LANES), 1)
    meta = jnp.zeros((e, LANES), F32)
    for t, st in enumerate(starts):
        meta = jnp.where(lane == META_P + t, st, meta)
    inv = 1.0 / chunk
    for t in range(cap // rt):
        first = jnp.sum(jnp.where(cin <= float(rt * t), 1.0, 0.0), axis=1, keepdims=True)
        lastt = jnp.sum(jnp.where(cin < float(rt * (t + 1)), 1.0, 0.0), axis=1, keepdims=True)
        meta = jnp.where(lane == META_LO + t, jnp.floor(first * inv), meta)
        meta = jnp.where(lane == META_HI + t, jnp.floor(lastt * inv), meta)
    meta_ref[0] = meta


def _route(aff, cap, rt, chunk):
    b, e, n = aff.shape
    assert n // chunk <= META_LO - META_P and cap // rt <= META_HI - META_LO
    return pl.pallas_call(
        functools.partial(_route_kernel, cap=cap, rt=rt, chunk=chunk),
        grid=(b,),
        in_specs=[pl.BlockSpec((1, e, n), lambda bi: (bi, 0, 0))],
        out_specs=[pl.BlockSpec((1, e, n), lambda bi: (bi, 0, 0)),
                   pl.BlockSpec((1, n, LANES), lambda bi: (bi, 0, 0)),
                   pl.BlockSpec((1, e, LANES), lambda bi: (bi, 0, 0))],
        out_shape=[jax.ShapeDtypeStruct((b, e, n), F32), jax.ShapeDtypeStruct((b, n, LANES), F32),
                   jax.ShapeDtypeStruct((b, e, LANES), F32)],
        compiler_params=_cparams(("parallel",)),
        name="route",
    )(aff)


def _ffn_kernel(lo_ref, hi_ref, posm_ref, aff_ref, h_ref, wg_ref, wu_ref, wd_ref, ys_ref, xs_scr, gate_scr,
                *, rt, chunk, ne, nt):
    bi, ei, ti = pl.program_id(0), pl.program_id(1), pl.program_id(2)
    lin = (bi * ne + ei) * nt + ti
    slot = (ti * rt + lax.broadcasted_iota(I32, (rt, 1), 0)).astype(F32)
    xs_scr[...] = jnp.zeros_like(xs_scr)
    gate_scr[...] = jnp.zeros_like(gate_scr)

    def body(c, carry):
        t0 = pl.multiple_of(c * chunk, chunk)
        hit = slot == posm_ref[0, :, pl.ds(t0, chunk)]
        xs_scr[...] += jnp.dot(jnp.where(hit, 1.0, 0.0).astype(BF16), h_ref[0, pl.ds(t0, chunk), :],
                               preferred_element_type=F32)
        gate_scr[...] += jnp.sum(jnp.where(hit, aff_ref[0, :, pl.ds(t0, chunk)], 0.0), axis=1, keepdims=True)
        return carry

    lax.fori_loop(lo_ref[lin], hi_ref[lin] + 1, body, 0)
    x = xs_scr[...].astype(BF16)
    a = jnp.dot(x, wg_ref[0], preferred_element_type=F32)
    u = jnp.dot(x, wu_ref[0], preferred_element_type=F32)
    act = (a * jax.nn.sigmoid(a) * u).astype(BF16)
    y = jnp.dot(act, wd_ref[0], preferred_element_type=F32)
    ys_ref[0, 0] = (y * gate_scr[...]).astype(BF16)


def _ffn(lo, hi, posm, aff, h2, wg, wu, wd, cap, rt, chunk):
    b, n, d = h2.shape
    e, _, hid = wg.shape
    nt = cap // rt
    rowmap = lambda bi, ei, ti, lo_r, hi_r: (bi * e + ei, 0, 0)
    grid_spec = pltpu.PrefetchScalarGridSpec(
        num_scalar_prefetch=2,
        grid=(b, e, nt),
        in_specs=[
            pl.BlockSpec((1, 1, n), rowmap),
            pl.BlockSpec((1, 1, n), rowmap),
            pl.BlockSpec((1, n, d), lambda bi, ei, ti, lo_r, hi_r: (bi, 0, 0), pipeline_mode=pl.Buffered(1)),
            pl.BlockSpec((1, d, hid), lambda bi, ei, ti, lo_r, hi_r: (ei, 0, 0)),
            pl.BlockSpec((1, d, hid), lambda bi, ei, ti, lo_r, hi_r: (ei, 0, 0)),
            pl.BlockSpec((1, hid, d), lambda bi, ei, ti, lo_r, hi_r: (ei, 0, 0)),
        ],
        out_specs=pl.BlockSpec((1, 1, rt, d), lambda bi, ei, ti, lo_r, hi_r: (bi, ei, ti, 0)),
        scratch_shapes=[pltpu.VMEM((rt, d), F32), pltpu.VMEM((rt, 1), F32)],
    )
    return pl.pallas_call(
        functools.partial(_ffn_kernel, rt=rt, chunk=chunk, ne=e, nt=nt),
        grid_spec=grid_spec,
        out_shape=jax.ShapeDtypeStruct((b, e, cap, d), BF16),
        compiler_params=_cparams(("arbitrary", "arbitrary", "arbitrary")),
        name="ffn",
    )(lo, hi, posm.reshape(b * e, 1, n), aff.reshape(b * e, 1, n), h2, wg, wu, wd)


def _combine_kernel(p0_ref, tail_ref, post_ref, ys_ref, x_ref, gt2_ref, o_ref, acc_scr, *, ne, ntt, wm, wt):
    bi, ti, ei = pl.program_id(0), pl.program_id(1), pl.program_id(2)
    lin = (bi * ntt + ti) * ne + ei
    tt = x_ref.shape[1]

    @pl.when(ei == 0)
    def _():
        acc_scr[...] = jnp.zeros_like(acc_scr)

    pt = post_ref[0]
    lane = lax.broadcasted_iota(I32, pt.shape, 1)
    rel = jnp.sum(jnp.where(lane == ei, pt, 0.0), axis=1, keepdims=True) - p0_ref[lin].astype(F32)
    hit = rel == lax.broadcasted_iota(I32, (tt, wm), 1).astype(F32)
    acc_scr[...] += jnp.dot(jnp.where(hit, 1.0, 0.0).astype(BF16), ys_ref[0, 0, 0:wm, :], preferred_element_type=F32)

    if wt:
        @pl.when(tail_ref[lin] > 0)
        def _():
            hit2 = rel == (wm + lax.broadcasted_iota(I32, (tt, wt), 1)).astype(F32)
            acc_scr[...] += jnp.dot(jnp.where(hit2, 1.0, 0.0).astype(BF16), ys_ref[0, 0, wm:wm + wt, :],
                                    preferred_element_type=F32)

    @pl.when(ei == ne - 1)
    def _():
        o_ref[0] = x_ref[0] + gt2_ref[0] * acc_scr[...]


def _combine(p0, tail, post, ys, x1, gt2, tt, wm, wt):
    b, n, d = x1.shape
    e = ys.shape[1]
    ntt = n // tt
    w = wm + wt

    def ys_map(bi, ti, ei, p0_r, tail_r):
        return (bi, ei, pl.multiple_of(p0_r[(bi * ntt + ti) * e + ei], BF16_SUBLANES), 0)

    grid_spec = pltpu.PrefetchScalarGridSpec(
        num_scalar_prefetch=2,
        grid=(b, ntt, e),
        in_specs=[
            pl.BlockSpec((1, tt, LANES), lambda bi, ti, ei, p0_r, tail_r: (bi, ti, 0)),
            pl.BlockSpec((pl.Element(1), pl.Element(1), pl.Element(w), pl.Element(d)), ys_map),
            pl.BlockSpec((1, tt, d), lambda bi, ti, ei, p0_r, tail_r: (bi, ti, 0)),
            pl.BlockSpec((1, 1, d), lambda bi, ti, ei, p0_r, tail_r: (bi, 0, 0)),
        ],
        out_specs=pl.BlockSpec((1, tt, d), lambda bi, ti, ei, p0_r, tail_r: (bi, ti, 0)),
        scratch_shapes=[pltpu.VMEM((tt, d), F32)],
    )
    return pl.pallas_call(
        functools.partial(_combine_kernel, ne=e, ntt=ntt, wm=wm, wt=wt),
        grid_spec=grid_spec,
        out_shape=jax.ShapeDtypeStruct((b, n, d), F32),
        compiler_params=_cparams(("arbitrary", "arbitrary", "arbitrary")),
        name="combine",
    )(p0, tail, post, ys, x1, gt2)


def _moe(h2, aff, x1, gt2, wg, wu, wd):
    b, n, d = x1.shape
    e = aff.shape[1]
    cap = CAPACITY_FACTOR * n // e
    chunk = MXU_DIM
    rt = min(MXU_DIM, cap)
    assert n % chunk == 0 and cap % rt == 0
    posm, post, meta = _route(aff, cap, rt, chunk)
    nt, ntt = cap // rt, n // chunk
    lo = meta[:, :, META_LO:META_LO + nt].astype(I32).reshape(-1)
    hi = meta[:, :, META_HI:META_HI + nt].astype(I32).reshape(-1)
    ys = _ffn(lo, hi, posm, aff, h2, wg, wu, wd, cap, rt, chunk)

    wm = min(MXU_DIM, cap)
    wt = BF16_SUBLANES if cap >= MXU_DIM + BF16_SUBLANES else 0
    assert cap <= wm or wt > 0
    p = meta[:, :, META_P:META_P + ntt].astype(I32)
    p_next = jnp.concatenate([p[:, :, 1:], jnp.full((b, e, 1), cap, I32)], axis=2)
    p0 = jnp.clip((p // BF16_SUBLANES) * BF16_SUBLANES, 0, cap - (wm + wt))
    tail = (p_next > p0 + wm).astype(I32)
    p0 = jnp.transpose(p0, (0, 2, 1)).reshape(-1)
    tail = jnp.transpose(tail, (0, 2, 1)).reshape(-1)
    return _combine(p0, tail, post, ys, x1, gt2, chunk, wm, wt)


def _rope_tables(n):
    rows = n // GRID_W
    row = jnp.repeat(jnp.arange(rows, dtype=F32), GRID_W)
    col = jnp.tile(jnp.arange(GRID_W, dtype=F32), rows)
    n_freq = HEAD_DIM // 4
    inv = ROPE_BASE ** (-jnp.arange(n_freq, dtype=F32) / n_freq)
    ang = jnp.concatenate([row[:, None] * inv, col[:, None] * inv], axis=-1)
    cos, sin = jnp.cos(ang), jnp.sin(ang)
    cs = jnp.concatenate([cos, cos, cos, cos], axis=1)
    sn = jnp.concatenate([-sin, sin, -sin, sin], axis=1)
    return cs, sn


def kernel(x, c, ctx, c_ctx, w_ada, b_ada, g_mix, g_ffn, w_in, qg_a, kg_a, qg_b, kg_b, sink_b, conv_w, w_branch,
           w_out, w_router, w_e_gate, w_e_up, w_e_down):
    b, n, d = x.shape
    l_ctx = ctx.shape[1]
    depth = w_ada.shape[0]
    tm = 256
    tq = 256
    tk = 512
    assert n % tk == 0 and n >= tq + 2 * WINDOW and l_ctx % tq == 0

    rows = -(-(b + 1) // 8) * 8
    s = jnp.concatenate([c, c_ctx[None, :], jnp.zeros((rows - b - 1, d), F32)], axis=0)
    mod_all = _ada(s, w_ada, b_ada)

    cs, sn = _rope_tables(n)
    cs_c = jnp.ones((l_ctx, LANES), F32)
    sn_c = jnp.zeros((l_ctx, LANES), F32)
    scale = HEAD_DIM ** -0.5

    xc = ctx
    for l in range(depth):
        last = l == depth - 1
        mod = mod_all[l]
        sh1, sc1, gt1, sh2, sc2, gt2 = [m[:, None, :] for m in jnp.split(mod[:b], 6, axis=-1)]
        shc1, scc1, gtc1, shc2, scc2, gtc2 = [jnp.broadcast_to(m[None, None, :], (b, 1, d))
                                              for m in jnp.split(mod[b], 6)]
        w_bf = w_in[l].astype(BF16)
        gq = (jnp.concatenate([jnp.tile(qg_a[l], N_HEADS), jnp.tile(qg_b[l], N_HEADS)]) * scale)[None, :]
        gk = jnp.concatenate([jnp.tile(kg_a[l], N_KV), jnp.tile(kg_b[l], N_KV)])[None, :]
        gm = g_mix[l][None, :]
        gf = g_ffn[l][None, :]
        wb_bf = w_branch[l].astype(BF16)
        wo_bf = w_out[l].astype(BF16)
        wr_t = jnp.transpose(w_router[l])
        wg, wu, wd = w_e_gate[l].astype(BF16), w_e_up[l].astype(BF16), w_e_down[l].astype(BF16)
        sink = sink_b[l].reshape(N_KV, GROUP)

        def sink_col(t):
            return jnp.repeat(sink, t, axis=1)[:, :, None]

        qa_c, qb_c, kta_c, va_c, ktb_c, vb_c, bg_c, u_c, gts_c = _inproj(
            xc, shc1, scc1, gm, w_bf, gq, gk, cs_c, sn_c, min(tm, l_ctx))
        qa, qb, kta, va, ktb, vb, bg, u, gts = _inproj(x, sh1, sc1, gm, w_bf, gq, gk, cs, sn, tm)

        o_a = _attention(qa, kta, va, kta_c, va_c, None, "global", tq, tk)
        o_b = _attention(qb, ktb, vb, ktb_c, vb_c, sink_col(tq), "window", tq, tk)
        x1, h2, aff = _merge(o_a, o_b, bg, u, gts, x, gt1, conv_w[l], wb_bf, wo_bf, gf, sh2, sc2, wr_t, tm)
        x = _moe(h2, aff, x1, gt2, wg, wu, wd)

        if not last:
            tqc = min(tq, l_ctx)
            oc_a = _attention(qa_c, None, None, kta_c, va_c, None, "none", tqc, tk)
            oc_b = _attention(qb_c, None, None, ktb_c, vb_c, sink_col(tqc), "none", tqc, tk)
            xc1, h2c, affc = _merge(oc_a, oc_b, bg_c, u_c, gts_c, xc, gtc1, conv_w[l], wb_bf, wo_bf, gf,
                                    shc2, scc2, wr_t, min(tm, l_ctx))
            xc = _moe(h2c, affc, xc1, gtc2, wg, wu, wd)
    return x
```

```python
import functools

import jax
import jax.numpy as jnp
from jax import lax
from jax.experimental import pallas as pl
from jax.experimental.pallas import tpu as pltpu

F32 = jnp.float32
BF16 = jnp.bfloat16
I32 = jnp.int32

HEAD_DIM = 64
N_HEADS = 8
N_KV = 2
GROUP = N_HEADS // N_KV
BRANCH = 512
N_EXPERTS = 16
CAPACITY_FACTOR = 2
GRID_W = 64
WINDOW = 128
ROPE_BASE = 10000.0
EPS = 1e-6
NEG_INF = -1e30

KV_W = N_KV * HEAD_DIM
OFF_QA = 4 * KV_W
OFF_QB = OFF_QA + N_HEADS * HEAD_DIM
OFF_CONV = OFF_QB + N_HEADS * HEAD_DIM
OFF_GATE = OFF_CONV + 3 * BRANCH

LANES = 128
V_ROWS = 128
MXU_DIM = 256
BF16_SUBLANES = 16
VMEM_LIMIT = 56 * 1024 * 1024


def _cparams(sem):
    return pltpu.CompilerParams(dimension_semantics=sem, vmem_limit_bytes=VMEM_LIMIT)


def _ada_kernel(s_ref, w_ref, b_ref, o_ref):
    s = s_ref[...]
    s = s * jax.nn.sigmoid(s)
    o_ref[0] = jnp.dot(s, w_ref[0], preferred_element_type=F32,
                       precision=lax.Precision.HIGHEST) + b_ref[0]


def _ada(s, w_ada, b_ada):
    depth, d, d6 = w_ada.shape
    rows = s.shape[0]
    tn = 1536
    return pl.pallas_call(
        _ada_kernel,
        grid=(depth, d6 // tn),
        in_specs=[
            pl.BlockSpec((rows, d), lambda l, j: (0, 0)),
            pl.BlockSpec((1, d, tn), lambda l, j: (l, 0, j)),
            pl.BlockSpec((1, 1, tn), lambda l, j: (l, 0, j)),
        ],
        out_specs=pl.BlockSpec((1, rows, tn), lambda l, j: (l, 0, j)),
        out_shape=jax.ShapeDtypeStruct((depth, rows, d6), F32),
        compiler_params=_cparams(("arbitrary", "arbitrary")),
        name="ada",
    )(s, w_ada, b_ada.reshape(depth, 1, d6))


def _inproj_kernel(x_ref, sh_ref, sc_ref, g_ref, w_ref, gq_ref, gk_ref, cs_ref, sn_ref,
                   qta_ref, qtb_ref, ka_ref, vta_ref, kb_ref, vtb_ref, bg_ref, u_ref, gt_ref):
    x = x_ref[0]
    tm = x.shape[0]
    ms = jnp.mean(x * x, axis=-1, keepdims=True)
    h = (x * lax.rsqrt(ms + EPS)) * g_ref[...]
    h = h * (1.0 + sc_ref[0]) + sh_ref[0]
    hb = h.astype(BF16)

    def proj(a, b):
        return jnp.dot(hb, w_ref[:, a:b], preferred_element_type=F32)

    slab = 2 * LANES
    r = lax.broadcasted_iota(I32, (slab, slab), 0) // HEAD_DIM
    c = lax.broadcasted_iota(I32, (slab, slab), 1) // HEAD_DIM
    head_mean = jnp.where(r == c, 1.0 / HEAD_DIM, 0.0).astype(BF16)
    cs = jnp.concatenate([cs_ref[...], cs_ref[...]], axis=1)
    sn = jnp.concatenate([sn_ref[...], sn_ref[...]], axis=1)
    lane = lax.broadcasted_iota(I32, (tm, slab), 1)
    first_half = (lane % HEAD_DIM) < (HEAD_DIM // 2)

    def head_norm_rope(p, gain):
        msq = jnp.dot((p * p).astype(BF16), head_mean, preferred_element_type=F32)
        y = p * lax.rsqrt(msq + EPS) * gain
        swapped = jnp.where(first_half, pltpu.roll(y, slab - HEAD_DIM // 2, 1), pltpu.roll(y, HEAD_DIM // 2, 1))
        return y * cs + swapped * sn

    pkv = proj(0, OFF_QA)
    k = jnp.concatenate([pkv[:, 0:KV_W], pkv[:, 2 * KV_W:3 * KV_W]], axis=1)
    k = head_norm_rope(k, gk_ref[...]).astype(BF16)
    ka_ref[0] = k[:, 0:KV_W]
    kb_ref[0] = k[:, KV_W:2 * KV_W]

    vt = jnp.transpose(jnp.concatenate([pkv[:, KV_W:2 * KV_W], pkv[:, 3 * KV_W:4 * KV_W]], axis=1))
    sub = lax.broadcasted_iota(I32, (V_ROWS - HEAD_DIM, tm), 0)
    ones_rows = jnp.where(sub == 0, 1.0, 0.0)
    for hh, ref in enumerate((vta_ref, vta_ref, vtb_ref, vtb_ref)):
        ref[0, hh % N_KV] = jnp.concatenate([vt[hh * HEAD_DIM:(hh + 1) * HEAD_DIM], ones_rows], axis=0).astype(BF16)

    pq = proj(OFF_QA, OFF_CONV)
    for s in range(4):
        q = head_norm_rope(pq[:, s * slab:(s + 1) * slab], gq_ref[:, s * slab:(s + 1) * slab])
        qt = jnp.transpose(q).astype(BF16)
        ref = qta_ref if s < 2 else qtb_ref
        for hh in range(GROUP):
            ref[0, (s % 2) * GROUP + hh] = qt[hh * HEAD_DIM:(hh + 1) * HEAD_DIM]

    pc = proj(OFF_CONV, OFF_GATE)
    bg_ref[0] = pc[:, 0:BRANCH].astype(BF16)
    u_ref[0] = (pc[:, BRANCH:2 * BRANCH] * pc[:, 2 * BRANCH:3 * BRANCH]).astype(BF16)

    d = x.shape[1]
    for j in range(3):
        pg = proj(OFF_GATE + j * d, OFF_GATE + (j + 1) * d)
        gt_ref[0, :, j * d:(j + 1) * d] = jax.nn.sigmoid(pg).astype(BF16)


def _inproj(x, sh, sc, g, w_bf, gq, gk, cs, sn, tm):
    b, n, d = x.shape
    pw = w_bf.shape[1]
    nq = N_HEADS * HEAD_DIM
    bmap = lambda bi, i: (bi, i, 0)
    qt_shape = jax.ShapeDtypeStruct((b, N_HEADS, HEAD_DIM, n), BF16)
    k_shape = jax.ShapeDtypeStruct((b, n, KV_W), BF16)
    vt_shape = jax.ShapeDtypeStruct((b, N_KV, V_ROWS, n), BF16)
    out_shape = [
        qt_shape, qt_shape, k_shape, vt_shape, k_shape, vt_shape,
        jax.ShapeDtypeStruct((b, n, BRANCH), BF16), jax.ShapeDtypeStruct((b, n, BRANCH), BF16),
        jax.ShapeDtypeStruct((b, n, 3 * d), BF16),
    ]
    qt_spec = pl.BlockSpec((1, N_HEADS, HEAD_DIM, tm), lambda bi, i: (bi, 0, 0, i))
    k_spec = pl.BlockSpec((1, tm, KV_W), bmap)
    vt_spec = pl.BlockSpec((1, N_KV, V_ROWS, tm), lambda bi, i: (bi, 0, 0, i))
    out_specs = [
        qt_spec, qt_spec, k_spec, vt_spec, k_spec, vt_spec,
        pl.BlockSpec((1, tm, BRANCH), bmap), pl.BlockSpec((1, tm, BRANCH), bmap),
        pl.BlockSpec((1, tm, 3 * d), bmap),
    ]
    in_specs = [
        pl.BlockSpec((1, tm, d), bmap),
        pl.BlockSpec((1, 1, d), lambda bi, i: (bi, 0, 0)),
        pl.BlockSpec((1, 1, d), lambda bi, i: (bi, 0, 0)),
        pl.BlockSpec((1, d), lambda bi, i: (0, 0)),
        pl.BlockSpec((d, pw), lambda bi, i: (0, 0), pipeline_mode=pl.Buffered(1)),
        pl.BlockSpec((1, 2 * nq), lambda bi, i: (0, 0)),
        pl.BlockSpec((1, 2 * LANES), lambda bi, i: (0, 0)),
        pl.BlockSpec((tm, LANES), lambda bi, i: (i, 0)),
        pl.BlockSpec((tm, LANES), lambda bi, i: (i, 0)),
    ]
    return pl.pallas_call(
        _inproj_kernel,
        grid=(b, n // tm),
        in_specs=in_specs,
        out_specs=out_specs,
        out_shape=out_shape,
        compiler_params=_cparams(("parallel", "parallel")),
        name="inproj",
    )(x, sh, sc, g, w_bf, gq, gk, cs, sn)


def _attn_kernel(*refs, mode, has_sink, tq, tk, n_lat):
    it = iter(refs)
    qt_ref = next(it)
    if mode != "none":
        kl_ref = next(it)
        vtl_ref = next(it)
    if mode != "global":
        kc_ref = next(it)
        vtc_ref = next(it)
    sink_ref = next(it) if has_sink else None
    o_ref = next(it)
    qp_scr = next(it)
    m_scr = next(it)
    acc_scr = next(it)
    if mode == "global":
        s_scr = next(it)
        cmax_scr = next(it)

    kv = pl.program_id(1)
    i = pl.program_id(2)
    for g in range(GROUP):
        qt = qt_ref[0, g]
        zero = jnp.zeros_like(qt)
        qp_scr[0:HEAD_DIM, g * tq:(g + 1) * tq] = jnp.where(kv == 0, qt, zero)
        qp_scr[HEAD_DIM:2 * HEAD_DIM, g * tq:(g + 1) * tq] = jnp.where(kv == 0, zero, qt)
    if has_sink:
        m_scr[...] = sink_ref[0]
    else:
        m_scr[...] = jnp.full(m_scr.shape, NEG_INF, F32)
    acc_scr[...] = jnp.zeros(acc_scr.shape, F32)

    def lanes_of(g):
        return slice(g * tq, (g + 1) * tq)

    def scores(g, k):
        return jnp.dot(k, qp_scr[:, lanes_of(g)], preferred_element_type=F32)

    def softmax_pv(g, s, cmax, vt):
        sl = lanes_of(g)
        m_prev = m_scr[:, sl]
        m_new = jnp.maximum(m_prev, cmax)
        alpha = jnp.exp2(m_prev - m_new)
        p = jnp.exp2(s - m_new).astype(BF16)
        acc_scr[:, sl] = alpha * acc_scr[:, sl] + jnp.dot(vt, p, preferred_element_type=F32)
        m_scr[:, sl] = m_new

    if mode == "global":
        nt = n_lat // tk

        def keys(j):
            return kl_ref[0, pl.ds(pl.multiple_of(j * tk, tk), tk), :]

        def values_t(j):
            return vtl_ref[0, 0, :, pl.ds(pl.multiple_of(j * tk, tk), tk)]

        def issue_scores(g, k, par):
            s = scores(g, k)
            s_scr[par, g] = s
            cmax_scr[par, :, lanes_of(g)] = jnp.max(s, axis=0, keepdims=True)

        k0 = keys(0)
        for g in range(GROUP):
            issue_scores(g, k0, 0)

        def consume(j, par):
            vt = values_t(j)
            for g in range(GROUP):
                softmax_pv(g, s_scr[par, g], cmax_scr[par, :, lanes_of(g)], vt)

        def pipelined(j, par):
            k_next = keys(j + 1)
            for g in range(GROUP):
                issue_scores(g, k_next, 1 - par)
            consume(j, par)

        def body(jj, carry):
            pipelined(2 * jj, 0)
            pipelined(2 * jj + 1, 1)
            return carry

        lax.fori_loop(0, (nt - 1) // 2, body, 0)
        if (nt - 1) % 2:
            pipelined(nt - 2, 0)
        consume(nt - 1, (nt - 1) % 2)
    else:
        if mode == "window":
            span = tq + 2 * WINDOW
            start = pl.multiple_of(jnp.clip(i * tq - WINDOW, 0, n_lat - span), LANES)
            l_ctx = kc_ref.shape[1]
            k = jnp.concatenate([kl_ref[0, pl.ds(start, span), :], kc_ref[0]], axis=0)
            vt = jnp.concatenate([vtl_ref[0, 0, :, pl.ds(start, span)], vtc_ref[0, 0]], axis=1)
            row = lax.broadcasted_iota(I32, (span + l_ctx, tq), 0)
            qpos = i * tq + lax.broadcasted_iota(I32, (span + l_ctx, tq), 1)
            mask = (row >= span) | (jnp.abs(qpos - (start + row)) <= WINDOW)
        else:
            k, vt, mask = kc_ref[0], vtc_ref[0, 0], None
        all_scores = [scores(g, k) for g in range(GROUP)]
        for g in range(GROUP):
            s = all_scores[g] if mask is None else jnp.where(mask, all_scores[g], NEG_INF)
            softmax_pv(g, s, jnp.max(s, axis=0, keepdims=True), vt)

    acc = acc_scr[...]
    denom = acc[HEAD_DIM:HEAD_DIM + 1, :]
    if has_sink:
        denom = denom + jnp.exp2(sink_ref[0] - m_scr[...])
    ot = acc / denom
    lane = lax.broadcasted_iota(I32, (tq, LANES), 1)
    heads = [jnp.transpose(ot[:, g * tq:(g + 1) * tq]) for g in range(GROUP)]
    pairs = [jnp.where(lane < HEAD_DIM, heads[2 * j], pltpu.roll(heads[2 * j + 1], HEAD_DIM, 1))
             for j in range(GROUP // 2)]
    o_ref[0] = jnp.concatenate(pairs, axis=1).astype(BF16)


def _attention(qt, k_lat, vt_lat, k_ctx, vt_ctx, sink, mode, tq, tk):
    b, _, _, nq = qt.shape
    n_lat = k_lat.shape[1] if mode != "none" else 0
    gw = GROUP * HEAD_DIM
    lanes = GROUP * tq
    kmap = lambda bi, kv, i: (bi, 0, 0)
    vmap = lambda bi, kv, i: (bi, kv, 0, 0)
    in_specs = [pl.BlockSpec((1, GROUP, HEAD_DIM, tq), lambda bi, kv, i: (bi, kv, 0, i))]
    args = [qt]
    if mode != "none":
        in_specs += [pl.BlockSpec((1, n_lat, KV_W), kmap), pl.BlockSpec((1, 1, V_ROWS, n_lat), vmap)]
        args += [k_lat, vt_lat]
    if mode != "global":
        l_ctx = k_ctx.shape[1]
        in_specs += [pl.BlockSpec((1, l_ctx, KV_W), kmap), pl.BlockSpec((1, 1, V_ROWS, l_ctx), vmap)]
        args += [k_ctx, vt_ctx]
    if sink is not None:
        in_specs.append(pl.BlockSpec((1, 1, lanes), lambda bi, kv, i: (kv, 0, 0)))
        args.append(sink)
    scratch = [pltpu.VMEM((2 * HEAD_DIM, lanes), BF16), pltpu.VMEM((1, lanes), F32), pltpu.VMEM((V_ROWS, lanes), F32)]
    if mode == "global":
        assert n_lat % tk == 0
        scratch += [pltpu.VMEM((2, GROUP, tk, tq), F32), pltpu.VMEM((2, 1, lanes), F32)]
    kern = functools.partial(_attn_kernel, mode=mode, has_sink=sink is not None, tq=tq, tk=tk, n_lat=n_lat)
    return pl.pallas_call(
        kern,
        grid=(b, N_KV, nq // tq),
        in_specs=in_specs,
        out_specs=pl.BlockSpec((1, tq, gw), lambda bi, kv, i: (bi, i, kv)),
        out_shape=jax.ShapeDtypeStruct((b, nq, N_HEADS * HEAD_DIM), BF16),
        scratch_shapes=scratch,
        compiler_params=_cparams(("parallel", "parallel", "parallel")),
        name="attn_" + mode + ("_sink" if sink is not None else ""),
    )(*args)


def _merge_kernel(oa_ref, ob_ref, bg_ref, u_ref, up_ref, un_ref, gts_ref, x_ref, gt1_ref, cw_ref, wb_ref, wo_ref,
                  gf_ref, sh2_ref, sc2_ref, wr_ref, x1_ref, h2_ref, aff_ref, *, nt):
    i = pl.program_id(1)
    tm, d = x_ref.shape[1], x_ref.shape[2]
    u = u_ref[0].astype(F32)
    prev_row = jnp.where(i > 0, up_ref[0][BF16_SUBLANES - 1:BF16_SUBLANES, :].astype(F32), 0.0)
    next_row = jnp.where(i < nt - 1, un_ref[0][0:1, :].astype(F32), 0.0)
    row = lax.broadcasted_iota(I32, (tm, BRANCH), 0)
    um1 = jnp.where(row == 0, prev_row, pltpu.roll(u, 1, 0))
    up1 = jnp.where(row == tm - 1, next_row, pltpu.roll(u, tm - 1, 0))
    cw = cw_ref[...]
    conv = cw[0:1] * um1 + cw[1:2] * u + cw[2:3] * up1
    oc = (bg_ref[0].astype(F32) * conv).astype(BF16)

    merged = gts_ref[0, :, 0:d].astype(F32) * jnp.dot(oa_ref[0], wb_ref[0], preferred_element_type=F32)
    merged += gts_ref[0, :, d:2 * d].astype(F32) * jnp.dot(ob_ref[0], wb_ref[1], preferred_element_type=F32)
    merged += gts_ref[0, :, 2 * d:3 * d].astype(F32) * jnp.dot(oc, wb_ref[2], preferred_element_type=F32)
    y = jnp.dot(merged.astype(BF16), wo_ref[...], preferred_element_type=F32)
    x1 = x_ref[0] + gt1_ref[0] * y
    x1_ref[0] = x1

    ms = jnp.mean(x1 * x1, axis=-1, keepdims=True)
    h2 = (x1 * lax.rsqrt(ms + EPS)) * gf_ref[...]
    h2 = h2 * (1.0 + sc2_ref[0]) + sh2_ref[0]
    h_hi = h2.astype(BF16)
    h2_ref[0] = h_hi

    h_lo = (h2 - h_hi.astype(F32)).astype(BF16)
    wr = wr_ref[...]
    wr_hi = wr.astype(BF16)
    wr_lo = (wr - wr_hi.astype(F32)).astype(BF16)
    dn = (((1,), (1,)), ((), ()))
    lg = lax.dot_general(wr_hi, h_hi, dn, preferred_element_type=F32)
    lg += lax.dot_general(wr_hi, h_lo, dn, preferred_element_type=F32)
    lg += lax.dot_general(wr_lo, h_hi, dn, preferred_element_type=F32)
    ex = jnp.exp(lg - jnp.max(lg, axis=0, keepdims=True))
    aff_ref[0] = ex / jnp.sum(ex, axis=0, keepdims=True)


def _merge(oa, ob, bg, u, gts, x, gt1, cw, wb_bf, wo_bf, gf, sh2, sc2, wr_t, tm):
    b, n, d = x.shape
    nt = n // tm
    e = wr_t.shape[0]
    bmap = lambda bi, i: (bi, i, 0)
    vec = lambda bi, i: (bi, 0, 0)
    halo = BF16_SUBLANES
    per = tm // halo
    last = n // halo - 1
    in_specs = [
        pl.BlockSpec((1, tm, BRANCH), bmap), pl.BlockSpec((1, tm, BRANCH), bmap),
        pl.BlockSpec((1, tm, BRANCH), bmap), pl.BlockSpec((1, tm, BRANCH), bmap),
        pl.BlockSpec((1, halo, BRANCH), lambda bi, i: (bi, jnp.maximum(i * per - 1, 0), 0)),
        pl.BlockSpec((1, halo, BRANCH), lambda bi, i: (bi, jnp.minimum((i + 1) * per, last), 0)),
        pl.BlockSpec((1, tm, 3 * d), bmap),
        pl.BlockSpec((1, tm, d), bmap),
        pl.BlockSpec((1, 1, d), vec),
        pl.BlockSpec((3, BRANCH), lambda bi, i: (0, 0)),
        pl.BlockSpec((3, BRANCH, d), lambda bi, i: (0, 0, 0)),
        pl.BlockSpec((d, d), lambda bi, i: (0, 0)),
        pl.BlockSpec((1, d), lambda bi, i: (0, 0)),
        pl.BlockSpec((1, 1, d), vec), pl.BlockSpec((1, 1, d), vec),
        pl.BlockSpec((e, d), lambda bi, i: (0, 0)),
    ]
    out_specs = [pl.BlockSpec((1, tm, d), bmap), pl.BlockSpec((1, tm, d), bmap),
                 pl.BlockSpec((1, e, tm), lambda bi, i: (bi, 0, i))]
    out_shape = [jax.ShapeDtypeStruct((b, n, d), F32), jax.ShapeDtypeStruct((b, n, d), BF16),
                 jax.ShapeDtypeStruct((b, e, n), F32)]
    return pl.pallas_call(
        functools.partial(_merge_kernel, nt=nt),
        grid=(b, nt),
        in_specs=in_specs,
        out_specs=out_specs,
        out_shape=out_shape,
        compiler_params=_cparams(("parallel", "parallel")),
        name="merge",
    )(oa, ob, bg, u, u, u, gts, x, gt1, cw, wb_bf, wo_bf, gf, sh2, sc2, wr_t)


META_P, META_LO, META_HI = 0, 64, 96


def _cumsum_lanes(x, chunk):
    r = lax.broadcasted_iota(I32, (chunk, chunk), 0)
    c = lax.broadcasted_iota(I32, (chunk, chunk), 1)
    upper = jnp.where(r <= c, 1.0, 0.0).astype(BF16)
    carry = jnp.zeros((x.shape[0], 1), F32)
    outs, starts = [], []
    for j in range(x.shape[1] // chunk):
        starts.append(carry)
        y = jnp.dot(x[:, j * chunk:(j + 1) * chunk].astype(BF16), upper, preferred_element_type=F32) + carry
        outs.append(y)
        carry = y[:, chunk - 1:chunk]
    return jnp.concatenate(outs, axis=1), starts


def _route_kernel(aff_ref, posm_ref, post_ref, meta_ref, *, cap, rt, chunk):
    a = aff_ref[0]
    e, n = a.shape
    bits = pltpu.bitcast(a, I32)

    def search(it, cur):
        cand = cur | jnp.left_shift(jnp.int32(1), 30 - it)
        cnt = jnp.sum(jnp.where(bits >= cand, 1.0, 0.0), axis=1, keepdims=True)
        return jnp.where(cnt >= cap, cand, cur)

    tau = lax.fori_loop(0, 31, search, jnp.zeros((e, 1), I32))
    gt = bits > tau
    eq = bits == tau
    need = cap - jnp.sum(jnp.where(gt, 1.0, 0.0), axis=1, keepdims=True)
    eqf = jnp.where(eq, 1.0, 0.0)
    ceq, _ = _cumsum_lanes(eqf, chunk)
    sel = gt | (eq & ((ceq - eqf) < need))
    sf = jnp.where(sel, 1.0, 0.0)
    cin, starts = _cumsum_lanes(sf, chunk)
    posm = jnp.where(sel, cin - sf, -1.0)
    posm_ref[0] = posm
    pad = jnp.full((LANES - e, n), -1.0, F32)
    post_ref[0] = jnp.transpose(jnp.concatenate([posm, pad], axis=0))

    lane = lax.broadcasted_iota(I32, (e, LANES), 1)
    meta = jnp.zeros((e, LANES), F32)
    for t, st in enumerate(starts):
        meta = jnp.where(lane == META_P + t, st, meta)
    inv = 1.0 / chunk
    for t in range(cap // rt):
        first = jnp.sum(jnp.where(cin <= float(rt * t), 1.0, 0.0), axis=1, keepdims=True)
        lastt = jnp.sum(jnp.where(cin < float(rt * (t + 1)), 1.0, 0.0), axis=1, keepdims=True)
        meta = jnp.where(lane == META_LO + t, jnp.floor(first * inv), meta)
        meta = jnp.where(lane == META_HI + t, jnp.floor(lastt * inv), meta)
    meta_ref[0] = meta


def _route(aff, cap, rt, chunk):
    b, e, n = aff.shape
    assert n // chunk <= META_LO - META_P and cap // rt <= META_HI - META_LO
    return pl.pallas_call(
        functools.partial(_route_kernel, cap=cap, rt=rt, chunk=chunk),
        grid=(b,),
        in_specs=[pl.BlockSpec((1, e, n), lambda bi: (bi, 0, 0))],
        out_specs=[pl.BlockSpec((1, e, n), lambda bi: (bi, 0, 0)),
                   pl.BlockSpec((1, n, LANES), lambda bi: (bi, 0, 0)),
                   pl.BlockSpec((1, e, LANES), lambda bi: (bi, 0, 0))],
        out_shape=[jax.ShapeDtypeStruct((b, e, n), F32), jax.ShapeDtypeStruct((b, n, LANES), F32),
                   jax.ShapeDtypeStruct((b, e, LANES), F32)],
        compiler_params=_cparams(("parallel",)),
        name="route",
    )(aff)


def _ffn_kernel(lo_ref, hi_ref, posm_ref, aff_ref, h_ref, wg_ref, wu_ref, wd_ref, ys_ref, xs_scr, gate_scr,
                *, rt, chunk, ne, nt):
    bi, ei, ti = pl.program_id(0), pl.program_id(1), pl.program_id(2)
    lin = (bi * ne + ei) * nt + ti
    slot = (ti * rt + lax.broadcasted_iota(I32, (rt, 1), 0)).astype(F32)
    xs_scr[...] = jnp.zeros_like(xs_scr)
    gate_scr[...] = jnp.zeros_like(gate_scr)

    def body(c, carry):
        t0 = pl.multiple_of(c * chunk, chunk)
        hit = slot == posm_ref[0, :, pl.ds(t0, chunk)]
        xs_scr[...] += jnp.dot(jnp.where(hit, 1.0, 0.0).astype(BF16), h_ref[0, pl.ds(t0, chunk), :],
                               preferred_element_type=F32)
        gate_scr[...] += jnp.sum(jnp.where(hit, aff_ref[0, :, pl.ds(t0, chunk)], 0.0), axis=1, keepdims=True)
        return carry

    lax.fori_loop(lo_ref[lin], hi_ref[lin] + 1, body, 0)
    x = xs_scr[...].astype(BF16)
    a = jnp.dot(x, wg_ref[0], preferred_element_type=F32)
    u = jnp.dot(x, wu_ref[0], preferred_element_type=F32)
    act = (a * jax.nn.sigmoid(a) * u).astype(BF16)
    y = jnp.dot(act, wd_ref[0], preferred_element_type=F32)
    ys_ref[0, 0] = (y * gate_scr[...]).astype(BF16)


def _ffn(lo, hi, posm, aff, h2, wg, wu, wd, cap, rt, chunk):
    b, n, d = h2.shape
    e, _, hid = wg.shape
    nt = cap // rt
    rowmap = lambda bi, ei, ti, lo_r, hi_r: (bi * e + ei, 0, 0)
    grid_spec = pltpu.PrefetchScalarGridSpec(
        num_scalar_prefetch=2,
        grid=(b, e, nt),
        in_specs=[
            pl.BlockSpec((1, 1, n), rowmap),
            pl.BlockSpec((1, 1, n), rowmap),
            pl.BlockSpec((1, n, d), lambda bi, ei, ti, lo_r, hi_r: (bi, 0, 0), pipeline_mode=pl.Buffered(1)),
            pl.BlockSpec((1, d, hid), lambda bi, ei, ti, lo_r, hi_r: (ei, 0, 0)),
            pl.BlockSpec((1, d, hid), lambda bi, ei, ti, lo_r, hi_r: (ei, 0, 0)),
            pl.BlockSpec((1, hid, d), lambda bi, ei, ti, lo_r, hi_r: (ei, 0, 0)),
        ],
        out_specs=pl.BlockSpec((1, 1, rt, d), lambda bi, ei, ti, lo_r, hi_r: (bi, ei, ti, 0)),
        scratch_shapes=[pltpu.VMEM((rt, d), F32), pltpu.VMEM((rt, 1), F32)],
    )
    return pl.pallas_call(
        functools.partial(_ffn_kernel, rt=rt, chunk=chunk, ne=e, nt=nt),
        grid_spec=grid_spec,
        out_shape=jax.ShapeDtypeStruct((b, e, cap, d), BF16),
        compiler_params=_cparams(("arbitrary", "arbitrary", "arbitrary")),
        name="ffn",
    )(lo, hi, posm.reshape(b * e, 1, n), aff.reshape(b * e, 1, n), h2, wg, wu, wd)


def _combine_kernel(p0_ref, tail_ref, post_ref, ys_ref, x_ref, gt2_ref, o_ref, acc_scr, *, ne, ntt, wm, wt):
    bi, ti, ei = pl.program_id(0), pl.program_id(1), pl.program_id(2)
    lin = (bi * ntt + ti) * ne + ei
    tt = x_ref.shape[1]

    @pl.when(ei == 0)
    def _():
        acc_scr[...] = jnp.zeros_like(acc_scr)

    pt = post_ref[0]
    lane = lax.broadcasted_iota(I32, pt.shape, 1)
    rel = jnp.sum(jnp.where(lane == ei, pt, 0.0), axis=1, keepdims=True) - p0_ref[lin].astype(F32)
    hit = rel == lax.broadcasted_iota(I32, (tt, wm), 1).astype(F32)
    acc_scr[...] += jnp.dot(jnp.where(hit, 1.0, 0.0).astype(BF16), ys_ref[0, 0, 0:wm, :], preferred_element_type=F32)

    if wt:
        @pl.when(tail_ref[lin] > 0)
        def _():
            hit2 = rel == (wm + lax.broadcasted_iota(I32, (tt, wt), 1)).astype(F32)
            acc_scr[...] += jnp.dot(jnp.where(hit2, 1.0, 0.0).astype(BF16), ys_ref[0, 0, wm:wm + wt, :],
                                    preferred_element_type=F32)

    @pl.when(ei == ne - 1)
    def _():
        o_ref[0] = x_ref[0] + gt2_ref[0] * acc_scr[...]


def _combine(p0, tail, post, ys, x1, gt2, tt, wm, wt):
    b, n, d = x1.shape
    e = ys.shape[1]
    ntt = n // tt
    w = wm + wt

    def ys_map(bi, ti, ei, p0_r, tail_r):
        return (bi, ei, pl.multiple_of(p0_r[(bi * ntt + ti) * e + ei], BF16_SUBLANES), 0)

    grid_spec = pltpu.PrefetchScalarGridSpec(
        num_scalar_prefetch=2,
        grid=(b, ntt, e),
        in_specs=[
            pl.BlockSpec((1, tt, LANES), lambda bi, ti, ei, p0_r, tail_r: (bi, ti, 0)),
            pl.BlockSpec((pl.Element(1), pl.Element(1), pl.Element(w), pl.Element(d)), ys_map),
            pl.BlockSpec((1, tt, d), lambda bi, ti, ei, p0_r, tail_r: (bi, ti, 0)),
            pl.BlockSpec((1, 1, d), lambda bi, ti, ei, p0_r, tail_r: (bi, 0, 0)),
        ],
        out_specs=pl.BlockSpec((1, tt, d), lambda bi, ti, ei, p0_r, tail_r: (bi, ti, 0)),
        scratch_shapes=[pltpu.VMEM((tt, d), F32)],
    )
    return pl.pallas_call(
        functools.partial(_combine_kernel, ne=e, ntt=ntt, wm=wm, wt=wt),
        grid_spec=grid_spec,
        out_shape=jax.ShapeDtypeStruct((b, n, d), F32),
        compiler_params=_cparams(("arbitrary", "arbitrary", "arbitrary")),
        name="combine",
    )(p0, tail, post, ys, x1, gt2)


def _moe(h2, aff, x1, gt2, wg, wu, wd):
    b, n, d = x1.shape
    e = aff.shape[1]
    cap = CAPACITY_FACTOR * n // e
    chunk = MXU_DIM
    rt = min(MXU_DIM, cap)
    assert n % chunk == 0 and cap % rt == 0
    posm, post, meta = _route(aff, cap, rt, chunk)
    nt, ntt = cap // rt, n // chunk
    lo = meta[:, :, META_LO:META_LO + nt].astype(I32).reshape(-1)
    hi = meta[:, :, META_HI:META_HI + nt].astype(I32).reshape(-1)
    ys = _ffn(lo, hi, posm, aff, h2, wg, wu, wd, cap, rt, chunk)

    wm = min(MXU_DIM, cap)
    wt = BF16_SUBLANES if cap >= MXU_DIM + BF16_SUBLANES else 0
    assert cap <= wm or wt > 0
    p = meta[:, :, META_P:META_P + ntt].astype(I32)
    p_next = jnp.concatenate([p[:, :, 1:], jnp.full((b, e, 1), cap, I32)], axis=2)
    p0 = jnp.clip((p // BF16_SUBLANES) * BF16_SUBLANES, 0, cap - (wm + wt))
    tail = (p_next > p0 + wm).astype(I32)
    p0 = jnp.transpose(p0, (0, 2, 1)).reshape(-1)
    tail = jnp.transpose(tail, (0, 2, 1)).reshape(-1)
    return _combine(p0, tail, post, ys, x1, gt2, chunk, wm, wt)


def _rope_tables(n):
    rows = n // GRID_W
    row = jnp.repeat(jnp.arange(rows, dtype=F32), GRID_W)
    col = jnp.tile(jnp.arange(GRID_W, dtype=F32), rows)
    n_freq = HEAD_DIM // 4
    inv = ROPE_BASE ** (-jnp.arange(n_freq, dtype=F32) / n_freq)
    ang = jnp.concatenate([row[:, None] * inv, col[:, None] * inv], axis=-1)
    cos, sin = jnp.cos(ang), jnp.sin(ang)
    cs = jnp.concatenate([cos, cos, cos, cos], axis=1)
    sn = jnp.concatenate([-sin, sin, -sin, sin], axis=1)
    return cs, sn


def kernel(x, c, ctx, c_ctx, w_ada, b_ada, g_mix, g_ffn, w_in, qg_a, kg_a, qg_b, kg_b, sink_b, conv_w, w_branch,
           w_out, w_router, w_e_gate, w_e_up, w_e_down):
    b, n, d = x.shape
    l_ctx = ctx.shape[1]
    depth = w_ada.shape[0]
    tm = 256
    tq = 256
    tk = next(t for t in (768, 512, 256, 128) if (n + l_ctx) % t == 0)
    assert n >= tq + 2 * WINDOW and l_ctx % tq == 0

    rows = -(-(b + 1) // 8) * 8
    s = jnp.concatenate([c, c_ctx[None, :], jnp.zeros((rows - b - 1, d), F32)], axis=0)
    mod_all = _ada(s, w_ada, b_ada)

    cs, sn = _rope_tables(n)
    cs_c = jnp.ones((l_ctx, LANES), F32)
    sn_c = jnp.zeros((l_ctx, LANES), F32)
    log2e = 1.4426950408889634
    scale = HEAD_DIM ** -0.5 * log2e

    xc = ctx
    for l in range(depth):
        last = l == depth - 1
        mod = mod_all[l]
        sh1, sc1, gt1, sh2, sc2, gt2 = [m[:, None, :] for m in jnp.split(mod[:b], 6, axis=-1)]
        shc1, scc1, gtc1, shc2, scc2, gtc2 = [jnp.broadcast_to(m[None, None, :], (b, 1, d))
                                              for m in jnp.split(mod[b], 6)]
        w_bf = w_in[l].astype(BF16)
        gq = (jnp.concatenate([jnp.tile(qg_a[l], N_HEADS), jnp.tile(qg_b[l], N_HEADS)]) * scale)[None, :]
        gk = jnp.concatenate([jnp.tile(kg_a[l], N_KV), jnp.tile(kg_b[l], N_KV)])[None, :]
        gm = g_mix[l][None, :]
        gf = g_ffn[l][None, :]
        wb_bf = w_branch[l].astype(BF16)
        wo_bf = w_out[l].astype(BF16)
        wr_t = jnp.transpose(w_router[l])
        wg, wu, wd = w_e_gate[l].astype(BF16), w_e_up[l].astype(BF16), w_e_down[l].astype(BF16)
        sink = sink_b[l].reshape(N_KV, GROUP) * log2e

        def sink_row(t):
            return jnp.repeat(sink, t, axis=1)[:, None, :]

        qa_c, qb_c, kta_c, va_c, ktb_c, vb_c, bg_c, u_c, gts_c = _inproj(
            xc, shc1, scc1, gm, w_bf, gq, gk, cs_c, sn_c, min(tm, l_ctx))
        qa, qb, kta, va, ktb, vb, bg, u, gts = _inproj(x, sh1, sc1, gm, w_bf, gq, gk, cs, sn, tm)

        k_all = jnp.concatenate([kta, kta_c], axis=1)
        vt_all = jnp.concatenate([va, va_c], axis=3)
        o_a = _attention(qa, k_all, vt_all, None, None, None, "global", tq, tk)
        o_b = _attention(qb, ktb, vb, ktb_c, vb_c, sink_row(tq), "window", tq, tk)
        x1, h2, aff = _merge(o_a, o_b, bg, u, gts, x, gt1, conv_w[l], wb_bf, wo_bf, gf, sh2, sc2, wr_t, tm)
        x = _moe(h2, aff, x1, gt2, wg, wu, wd)

        if not last:
            tqc = min(tq, l_ctx)
            oc_a = _attention(qa_c, None, None, kta_c, va_c, None, "none", tqc, tk)
            oc_b = _attention(qb_c, None, None, ktb_c, vb_c, sink_row(tqc), "none", tqc, tk)
            xc1, h2c, affc = _merge(oc_a, oc_b, bg_c, u_c, gts_c, xc, gtc1, conv_w[l], wb_bf, wo_bf, gf,
                                    shc2, scc2, wr_t, min(tm, l_ctx))
            xc = _moe(h2c, affc, xc1, gtc2, wg, wu, wd)
    return x
```

```python
import functools

import jax
import jax.numpy as jnp
from jax import lax
from jax.experimental import pallas as pl
from jax.experimental.pallas import tpu as pltpu

F32 = jnp.float32
BF16 = jnp.bfloat16
I32 = jnp.int32

HEAD_DIM = 64
N_HEADS = 8
N_KV = 2
GROUP = N_HEADS // N_KV
BRANCH = 512
N_EXPERTS = 16
CAPACITY_FACTOR = 2
GRID_W = 64
WINDOW = 128
ROPE_BASE = 10000.0
EPS = 1e-6
NEG_INF = -1e30

KV_W = N_KV * HEAD_DIM
OFF_QA = 4 * KV_W
OFF_QB = OFF_QA + N_HEADS * HEAD_DIM
OFF_CONV = OFF_QB + N_HEADS * HEAD_DIM
OFF_GATE = OFF_CONV + 3 * BRANCH

LANES = 128
V_ROWS = 80
MXU_DIM = 256
BF16_SUBLANES = 16
VMEM_LIMIT = 56 * 1024 * 1024


def _cparams(sem):
    return pltpu.CompilerParams(dimension_semantics=sem, vmem_limit_bytes=VMEM_LIMIT)


def _ada_kernel(s_ref, w_ref, b_ref, o_ref):
    s = s_ref[...]
    s = s * jax.nn.sigmoid(s)
    o_ref[0] = jnp.dot(s, w_ref[0], preferred_element_type=F32,
                       precision=lax.Precision.HIGHEST) + b_ref[0]


def _ada(s, w_ada, b_ada):
    depth, d, d6 = w_ada.shape
    rows = s.shape[0]
    tn = 1536
    return pl.pallas_call(
        _ada_kernel,
        grid=(depth, d6 // tn),
        in_specs=[
            pl.BlockSpec((rows, d), lambda l, j: (0, 0)),
            pl.BlockSpec((1, d, tn), lambda l, j: (l, 0, j)),
            pl.BlockSpec((1, 1, tn), lambda l, j: (l, 0, j)),
        ],
        out_specs=pl.BlockSpec((1, rows, tn), lambda l, j: (l, 0, j)),
        out_shape=jax.ShapeDtypeStruct((depth, rows, d6), F32),
        compiler_params=_cparams(("arbitrary", "arbitrary")),
        name="ada",
    )(s, w_ada, b_ada.reshape(depth, 1, d6))


def _inproj_kernel(x_ref, sh_ref, sc_ref, g_ref, w_ref, gq_ref, gk_ref, cs_ref, sn_ref,
                   qta_ref, qtb_ref, ka_ref, vta_ref, kb_ref, vtb_ref, bg_ref, u_ref, gt_ref):
    x = x_ref[0]
    tm = x.shape[0]
    ms = jnp.mean(x * x, axis=-1, keepdims=True)
    h = (x * lax.rsqrt(ms + EPS)) * g_ref[...]
    h = h * (1.0 + sc_ref[0]) + sh_ref[0]
    hb = h.astype(BF16)

    def proj(a, b):
        return jnp.dot(hb, w_ref[:, a:b], preferred_element_type=F32)

    slab = 2 * LANES
    r = lax.broadcasted_iota(I32, (slab, slab), 0) // HEAD_DIM
    c = lax.broadcasted_iota(I32, (slab, slab), 1) // HEAD_DIM
    head_mean = jnp.where(r == c, 1.0 / HEAD_DIM, 0.0).astype(BF16)
    cs = jnp.concatenate([cs_ref[...], cs_ref[...]], axis=1)
    sn = jnp.concatenate([sn_ref[...], sn_ref[...]], axis=1)
    lane = lax.broadcasted_iota(I32, (tm, slab), 1)
    first_half = (lane % HEAD_DIM) < (HEAD_DIM // 2)

    def head_norm_rope(p, gain):
        msq = jnp.dot((p * p).astype(BF16), head_mean, preferred_element_type=F32)
        y = p * lax.rsqrt(msq + EPS) * gain
        swapped = jnp.where(first_half, pltpu.roll(y, slab - HEAD_DIM // 2, 1), pltpu.roll(y, HEAD_DIM // 2, 1))
        return y * cs + swapped * sn

    pkv = proj(0, OFF_QA)
    k = jnp.concatenate([pkv[:, 0:KV_W], pkv[:, 2 * KV_W:3 * KV_W]], axis=1)
    k = head_norm_rope(k, gk_ref[...]).astype(BF16)
    ka_ref[0] = k[:, 0:KV_W]
    kb_ref[0] = k[:, KV_W:2 * KV_W]

    vt = jnp.transpose(jnp.concatenate([pkv[:, KV_W:2 * KV_W], pkv[:, 3 * KV_W:4 * KV_W]], axis=1))
    sub = lax.broadcasted_iota(I32, (V_ROWS - HEAD_DIM, tm), 0)
    ones_rows = jnp.where(sub == 0, 1.0, 0.0)
    for hh, ref in enumerate((vta_ref, vta_ref, vtb_ref, vtb_ref)):
        ref[0, hh % N_KV] = jnp.concatenate([vt[hh * HEAD_DIM:(hh + 1) * HEAD_DIM], ones_rows], axis=0).astype(BF16)

    pq = proj(OFF_QA, OFF_CONV)
    for s in range(4):
        q = head_norm_rope(pq[:, s * slab:(s + 1) * slab], gq_ref[:, s * slab:(s + 1) * slab])
        qt = jnp.transpose(q).astype(BF16)
        ref = qta_ref if s < 2 else qtb_ref
        for hh in range(GROUP):
            ref[0, (s % 2) * GROUP + hh] = qt[hh * HEAD_DIM:(hh + 1) * HEAD_DIM]

    pc = proj(OFF_CONV, OFF_GATE)
    bg_ref[0] = pc[:, 0:BRANCH].astype(BF16)
    u_ref[0] = (pc[:, BRANCH:2 * BRANCH] * pc[:, 2 * BRANCH:3 * BRANCH]).astype(BF16)

    d = x.shape[1]
    for j in range(3):
        pg = proj(OFF_GATE + j * d, OFF_GATE + (j + 1) * d)
        gt_ref[0, :, j * d:(j + 1) * d] = jax.nn.sigmoid(pg).astype(BF16)


def _inproj(x, sh, sc, g, w_bf, gq, gk, cs, sn, tm):
    b, n, d = x.shape
    pw = w_bf.shape[1]
    nq = N_HEADS * HEAD_DIM
    bmap = lambda bi, i: (bi, i, 0)
    qt_shape = jax.ShapeDtypeStruct((b, N_HEADS, HEAD_DIM, n), BF16)
    k_shape = jax.ShapeDtypeStruct((b, n, KV_W), BF16)
    vt_shape = jax.ShapeDtypeStruct((b, N_KV, V_ROWS, n), BF16)
    out_shape = [
        qt_shape, qt_shape, k_shape, vt_shape, k_shape, vt_shape,
        jax.ShapeDtypeStruct((b, n, BRANCH), BF16), jax.ShapeDtypeStruct((b, n, BRANCH), BF16),
        jax.ShapeDtypeStruct((b, n, 3 * d), BF16),
    ]
    qt_spec = pl.BlockSpec((1, N_HEADS, HEAD_DIM, tm), lambda bi, i: (bi, 0, 0, i))
    k_spec = pl.BlockSpec((1, tm, KV_W), bmap)
    vt_spec = pl.BlockSpec((1, N_KV, V_ROWS, tm), lambda bi, i: (bi, 0, 0, i))
    out_specs = [
        qt_spec, qt_spec, k_spec, vt_spec, k_spec, vt_spec,
        pl.BlockSpec((1, tm, BRANCH), bmap), pl.BlockSpec((1, tm, BRANCH), bmap),
        pl.BlockSpec((1, tm, 3 * d), bmap),
    ]
    in_specs = [
        pl.BlockSpec((1, tm, d), bmap),
        pl.BlockSpec((1, 1, d), lambda bi, i: (bi, 0, 0)),
        pl.BlockSpec((1, 1, d), lambda bi, i: (bi, 0, 0)),
        pl.BlockSpec((1, d), lambda bi, i: (0, 0)),
        pl.BlockSpec((d, pw), lambda bi, i: (0, 0), pipeline_mode=pl.Buffered(1)),
        pl.BlockSpec((1, 2 * nq), lambda bi, i: (0, 0)),
        pl.BlockSpec((1, 2 * LANES), lambda bi, i: (0, 0)),
        pl.BlockSpec((tm, LANES), lambda bi, i: (i, 0)),
        pl.BlockSpec((tm, LANES), lambda bi, i: (i, 0)),
    ]
    return pl.pallas_call(
        _inproj_kernel,
        grid=(b, n // tm),
        in_specs=in_specs,
        out_specs=out_specs,
        out_shape=out_shape,
        compiler_params=_cparams(("parallel", "parallel")),
        name="inproj",
    )(x, sh, sc, g, w_bf, gq, gk, cs, sn)


def _attn_kernel(*refs, mode, has_sink, tq, tk, n_lat):
    it = iter(refs)
    qt_ref = next(it)
    if mode != "none":
        kl_ref = next(it)
        vtl_ref = next(it)
    if mode != "global":
        kc_ref = next(it)
        vtc_ref = next(it)
    sink_ref = next(it) if has_sink else None
    o_ref = next(it)
    qp_scr = next(it)
    m_scr = next(it)
    acc_scr = next(it)
    if mode == "global":
        s_scr = next(it)
        cmax_scr = next(it)

    kv = pl.program_id(1)
    i = pl.program_id(2)
    for g in range(GROUP):
        qt = qt_ref[0, g]
        zero = jnp.zeros_like(qt)
        qp_scr[0:HEAD_DIM, g * tq:(g + 1) * tq] = jnp.where(kv == 0, qt, zero)
        qp_scr[HEAD_DIM:2 * HEAD_DIM, g * tq:(g + 1) * tq] = jnp.where(kv == 0, zero, qt)
    if has_sink:
        m_scr[...] = sink_ref[0]
    else:
        m_scr[...] = jnp.full(m_scr.shape, NEG_INF, F32)
    acc_scr[...] = jnp.zeros(acc_scr.shape, F32)

    def lanes_of(g):
        return slice(g * tq, (g + 1) * tq)

    def scores(g, k):
        return jnp.dot(k, qp_scr[:, lanes_of(g)], preferred_element_type=F32)

    def softmax_pv(g, s, cmax, vt):
        sl = lanes_of(g)
        m_prev = m_scr[:, sl]
        m_new = jnp.maximum(m_prev, cmax)
        alpha = jnp.exp2(m_prev - m_new)
        p = jnp.exp2(s - m_new).astype(BF16)
        acc_scr[:, sl] = alpha * acc_scr[:, sl] + jnp.dot(vt, p, preferred_element_type=F32)
        m_scr[:, sl] = m_new

    if mode == "global":
        nt = n_lat // tk

        def keys(j):
            return kl_ref[0, pl.ds(pl.multiple_of(j * tk, tk), tk), :]

        def values_t(j):
            return vtl_ref[0, 0, :, pl.ds(pl.multiple_of(j * tk, tk), tk)]

        def issue_scores(g, k, par):
            s = scores(g, k)
            s_scr[par, g] = s
            cmax_scr[par, :, lanes_of(g)] = jnp.max(s, axis=0, keepdims=True)

        k0 = keys(0)
        for g in range(GROUP):
            issue_scores(g, k0, 0)

        def consume(j, par):
            vt = values_t(j)
            for g in range(GROUP):
                softmax_pv(g, s_scr[par, g], cmax_scr[par, :, lanes_of(g)], vt)

        def pipelined(j, par):
            k_next = keys(j + 1)
            for g in range(GROUP):
                issue_scores(g, k_next, 1 - par)
            consume(j, par)

        def body(jj, carry):
            pipelined(2 * jj, 0)
            pipelined(2 * jj + 1, 1)
            return carry

        lax.fori_loop(0, (nt - 1) // 2, body, 0)
        if (nt - 1) % 2:
            pipelined(nt - 2, 0)
        consume(nt - 1, (nt - 1) % 2)
    else:
        if mode == "window":
            span = tq + 2 * WINDOW
            start = pl.multiple_of(jnp.clip(i * tq - WINDOW, 0, n_lat - span), LANES)
            l_ctx = kc_ref.shape[1]
            k = jnp.concatenate([kl_ref[0, pl.ds(start, span), :], kc_ref[0]], axis=0)
            vt = jnp.concatenate([vtl_ref[0, 0, :, pl.ds(start, span)], vtc_ref[0, 0]], axis=1)
            row = lax.broadcasted_iota(I32, (span + l_ctx, tq), 0)
            qpos = i * tq + lax.broadcasted_iota(I32, (span + l_ctx, tq), 1)
            mask = (row >= span) | (jnp.abs(qpos - (start + row)) <= WINDOW)
        else:
            k, vt, mask = kc_ref[0], vtc_ref[0, 0], None
        all_scores = [scores(g, k) for g in range(GROUP)]
        for g in range(GROUP):
            s = all_scores[g] if mask is None else jnp.where(mask, all_scores[g], NEG_INF)
            softmax_pv(g, s, jnp.max(s, axis=0, keepdims=True), vt)

    acc = acc_scr[...]
    denom = acc[HEAD_DIM:HEAD_DIM + 1, :]
    if has_sink:
        denom = denom + jnp.exp2(sink_ref[0] - m_scr[...])
    ot = acc / denom
    ot = jnp.concatenate([ot, jnp.zeros((LANES - V_ROWS, ot.shape[1]), F32)], axis=0)
    lane = lax.broadcasted_iota(I32, (tq, LANES), 1)
    heads = [jnp.transpose(ot[:, g * tq:(g + 1) * tq]) for g in range(GROUP)]
    pairs = [jnp.where(lane < HEAD_DIM, heads[2 * j], pltpu.roll(heads[2 * j + 1], HEAD_DIM, 1))
             for j in range(GROUP // 2)]
    o_ref[0] = jnp.concatenate(pairs, axis=1).astype(BF16)


def _attention(qt, k_lat, vt_lat, k_ctx, vt_ctx, sink, mode, tq, tk):
    b, _, _, nq = qt.shape
    n_lat = k_lat.shape[1] if mode != "none" else 0
    gw = GROUP * HEAD_DIM
    lanes = GROUP * tq
    kmap = lambda bi, kv, i: (bi, 0, 0)
    vmap = lambda bi, kv, i: (bi, kv, 0, 0)
    in_specs = [pl.BlockSpec((1, GROUP, HEAD_DIM, tq), lambda bi, kv, i: (bi, kv, 0, i))]
    args = [qt]
    if mode != "none":
        in_specs += [pl.BlockSpec((1, n_lat, KV_W), kmap), pl.BlockSpec((1, 1, V_ROWS, n_lat), vmap)]
        args += [k_lat, vt_lat]
    if mode != "global":
        l_ctx = k_ctx.shape[1]
        in_specs += [pl.BlockSpec((1, l_ctx, KV_W), kmap), pl.BlockSpec((1, 1, V_ROWS, l_ctx), vmap)]
        args += [k_ctx, vt_ctx]
    if sink is not None:
        in_specs.append(pl.BlockSpec((1, 1, lanes), lambda bi, kv, i: (kv, 0, 0)))
        args.append(sink)
    scratch = [pltpu.VMEM((2 * HEAD_DIM, lanes), BF16), pltpu.VMEM((1, lanes), F32), pltpu.VMEM((V_ROWS, lanes), F32)]
    if mode == "global":
        assert n_lat % tk == 0
        scratch += [pltpu.VMEM((2, GROUP, tk, tq), F32), pltpu.VMEM((2, 1, lanes), F32)]
    kern = functools.partial(_attn_kernel, mode=mode, has_sink=sink is not None, tq=tq, tk=tk, n_lat=n_lat)
    return pl.pallas_call(
        kern,
        grid=(b, N_KV, nq // tq),
        in_specs=in_specs,
        out_specs=pl.BlockSpec((1, tq, gw), lambda bi, kv, i: (bi, i, kv)),
        out_shape=jax.ShapeDtypeStruct((b, nq, N_HEADS * HEAD_DIM), BF16),
        scratch_shapes=scratch,
        compiler_params=_cparams(("parallel", "parallel", "parallel")),
        name="attn_" + mode + ("_sink" if sink is not None else ""),
    )(*args)


def _merge_kernel(oa_ref, ob_ref, bg_ref, u_ref, up_ref, un_ref, gts_ref, x_ref, gt1_ref, cw_ref, wb_ref, wo_ref,
                  gf_ref, sh2_ref, sc2_ref, wr_ref, x1_ref, h2_ref, aff_ref, *, nt):
    i = pl.program_id(1)
    tm, d = x_ref.shape[1], x_ref.shape[2]
    u = u_ref[0].astype(F32)
    prev_row = jnp.where(i > 0, up_ref[0][BF16_SUBLANES - 1:BF16_SUBLANES, :].astype(F32), 0.0)
    next_row = jnp.where(i < nt - 1, un_ref[0][0:1, :].astype(F32), 0.0)
    row = lax.broadcasted_iota(I32, (tm, BRANCH), 0)
    um1 = jnp.where(row == 0, prev_row, pltpu.roll(u, 1, 0))
    up1 = jnp.where(row == tm - 1, next_row, pltpu.roll(u, tm - 1, 0))
    cw = cw_ref[...]
    conv = cw[0:1] * um1 + cw[1:2] * u + cw[2:3] * up1
    oc = (bg_ref[0].astype(F32) * conv).astype(BF16)

    merged = gts_ref[0, :, 0:d].astype(F32) * jnp.dot(oa_ref[0], wb_ref[0], preferred_element_type=F32)
    merged += gts_ref[0, :, d:2 * d].astype(F32) * jnp.dot(ob_ref[0], wb_ref[1], preferred_element_type=F32)
    merged += gts_ref[0, :, 2 * d:3 * d].astype(F32) * jnp.dot(oc, wb_ref[2], preferred_element_type=F32)
    y = jnp.dot(merged.astype(BF16), wo_ref[...], preferred_element_type=F32)
    x1 = x_ref[0] + gt1_ref[0] * y
    x1_ref[0] = x1

    ms = jnp.mean(x1 * x1, axis=-1, keepdims=True)
    h2 = (x1 * lax.rsqrt(ms + EPS)) * gf_ref[...]
    h2 = h2 * (1.0 + sc2_ref[0]) + sh2_ref[0]
    h_hi = h2.astype(BF16)
    h2_ref[0] = h_hi

    h_lo = (h2 - h_hi.astype(F32)).astype(BF16)
    wr = wr_ref[...]
    wr_hi = wr.astype(BF16)
    wr_lo = (wr - wr_hi.astype(F32)).astype(BF16)
    dn = (((1,), (1,)), ((), ()))
    lg = lax.dot_general(wr_hi, h_hi, dn, preferred_element_type=F32)
    lg += lax.dot_general(wr_hi, h_lo, dn, preferred_element_type=F32)
    lg += lax.dot_general(wr_lo, h_hi, dn, preferred_element_type=F32)
    ex = jnp.exp(lg - jnp.max(lg, axis=0, keepdims=True))
    aff_ref[0] = ex / jnp.sum(ex, axis=0, keepdims=True)


def _merge(oa, ob, bg, u, gts, x, gt1, cw, wb_bf, wo_bf, gf, sh2, sc2, wr_t, tm):
    b, n, d = x.shape
    nt = n // tm
    e = wr_t.shape[0]
    bmap = lambda bi, i: (bi, i, 0)
    vec = lambda bi, i: (bi, 0, 0)
    halo = BF16_SUBLANES
    per = tm // halo
    last = n // halo - 1
    in_specs = [
        pl.BlockSpec((1, tm, BRANCH), bmap), pl.BlockSpec((1, tm, BRANCH), bmap),
        pl.BlockSpec((1, tm, BRANCH), bmap), pl.BlockSpec((1, tm, BRANCH), bmap),
        pl.BlockSpec((1, halo, BRANCH), lambda bi, i: (bi, jnp.maximum(i * per - 1, 0), 0)),
        pl.BlockSpec((1, halo, BRANCH), lambda bi, i: (bi, jnp.minimum((i + 1) * per, last), 0)),
        pl.BlockSpec((1, tm, 3 * d), bmap),
        pl.BlockSpec((1, tm, d), bmap),
        pl.BlockSpec((1, 1, d), vec),
        pl.BlockSpec((3, BRANCH), lambda bi, i: (0, 0)),
        pl.BlockSpec((3, BRANCH, d), lambda bi, i: (0, 0, 0)),
        pl.BlockSpec((d, d), lambda bi, i: (0, 0)),
        pl.BlockSpec((1, d), lambda bi, i: (0, 0)),
        pl.BlockSpec((1, 1, d), vec), pl.BlockSpec((1, 1, d), vec),
        pl.BlockSpec((e, d), lambda bi, i: (0, 0)),
    ]
    out_specs = [pl.BlockSpec((1, tm, d), bmap), pl.BlockSpec((1, tm, d), bmap),
                 pl.BlockSpec((1, e, tm), lambda bi, i: (bi, 0, i))]
    out_shape = [jax.ShapeDtypeStruct((b, n, d), F32), jax.ShapeDtypeStruct((b, n, d), BF16),
                 jax.ShapeDtypeStruct((b, e, n), F32)]
    return pl.pallas_call(
        functools.partial(_merge_kernel, nt=nt),
        grid=(b, nt),
        in_specs=in_specs,
        out_specs=out_specs,
        out_shape=out_shape,
        compiler_params=_cparams(("parallel", "parallel")),
        name="merge",
    )(oa, ob, bg, u, u, u, gts, x, gt1, cw, wb_bf, wo_bf, gf, sh2, sc2, wr_t)


META_P, META_LO, META_HI = 0, 64, 96


def _cumsum_lanes(x, chunk):
    r = lax.broadcasted_iota(I32, (chunk, chunk), 0)
    c = lax.broadcasted_iota(I32, (chunk, chunk), 1)
    upper = jnp.where(r <= c, 1.0, 0.0).astype(BF16)
    carry = jnp.zeros((x.shape[0], 1), F32)
    outs, starts = [], []
    for j in range(x.shape[1] // chunk):
        starts.append(carry)
        y = jnp.dot(x[:, j * chunk:(j + 1) * chunk].astype(BF16), upper, preferred_element_type=F32) + carry
        outs.append(y)
        carry = y[:, chunk - 1:chunk]
    return jnp.concatenate(outs, axis=1), starts


def _route_kernel(aff_ref, posm_ref, post_ref, meta_ref, *, cap, rt, chunk, gchunk):
    a = aff_ref[0]
    e, n = a.shape
    bits = pltpu.bitcast(a, I32)

    def search(it, cur):
        cand = cur | jnp.left_shift(jnp.int32(1), 30 - it)
        cnt = jnp.sum(jnp.where(bits >= cand, 1.0, 0.0), axis=1, keepdims=True)
        return jnp.where(cnt >= cap, cand, cur)

    tau = lax.fori_loop(0, 31, search, jnp.zeros((e, 1), I32))
    gt = bits > tau
    eq = bits == tau
    need = cap - jnp.sum(jnp.where(gt, 1.0, 0.0), axis=1, keepdims=True)
    eqf = jnp.where(eq, 1.0, 0.0)
    ceq, _ = _cumsum_lanes(eqf, chunk)
    sel = gt | (eq & ((ceq - eqf) < need))
    sf = jnp.where(sel, 1.0, 0.0)
    cin, starts = _cumsum_lanes(sf, chunk)
    posm = jnp.where(sel, cin - sf, -1.0)
    posm_ref[0] = posm
    pad = jnp.full((LANES - e, n), -1.0, F32)
    post_ref[0] = jnp.transpose(jnp.concatenate([posm, pad], axis=0))

    lane = lax.broadcasted_iota(I32, (e, LANES), 1)
    meta = jnp.zeros((e, LANES), F32)
    for t, st in enumerate(starts):
        meta = jnp.where(lane == META_P + t, st, meta)
    inv = 1.0 / gchunk
    for t in range(cap // rt):
        first = jnp.sum(jnp.where(cin <= float(rt * t), 1.0, 0.0), axis=1, keepdims=True)
        lastt = jnp.sum(jnp.where(cin < float(rt * (t + 1)), 1.0, 0.0), axis=1, keepdims=True)
        meta = jnp.where(lane == META_LO + t, jnp.floor(first * inv), meta)
        meta = jnp.where(lane == META_HI + t, jnp.floor(lastt * inv), meta)
    meta_ref[0] = meta


def _route(aff, cap, rt, chunk, gchunk):
    b, e, n = aff.shape
    assert n // chunk <= META_LO - META_P and cap // rt <= META_HI - META_LO
    return pl.pallas_call(
        functools.partial(_route_kernel, cap=cap, rt=rt, chunk=chunk, gchunk=gchunk),
        grid=(b,),
        in_specs=[pl.BlockSpec((1, e, n), lambda bi: (bi, 0, 0))],
        out_specs=[pl.BlockSpec((1, e, n), lambda bi: (bi, 0, 0)),
                   pl.BlockSpec((1, n, LANES), lambda bi: (bi, 0, 0)),
                   pl.BlockSpec((1, e, LANES), lambda bi: (bi, 0, 0))],
        out_shape=[jax.ShapeDtypeStruct((b, e, n), F32), jax.ShapeDtypeStruct((b, n, LANES), F32),
                   jax.ShapeDtypeStruct((b, e, LANES), F32)],
        compiler_params=_cparams(("parallel",)),
        name="route",
    )(aff)


def _ffn_kernel(lo_ref, hi_ref, posm_ref, aff_ref, h_ref, wg_ref, wu_ref, wd_ref, ys_ref, xs_scr, gate_scr,
                *, rt, chunk, ne, nt):
    bi, ei, ti = pl.program_id(0), pl.program_id(1), pl.program_id(2)
    lin = (bi * ne + ei) * nt + ti
    slot = (ti * rt + lax.broadcasted_iota(I32, (rt, 1), 0)).astype(F32)
    xs_scr[...] = jnp.zeros_like(xs_scr)
    gate_scr[...] = jnp.zeros_like(gate_scr)

    def body(c, carry):
        t0 = pl.multiple_of(c * chunk, chunk)
        hit = slot == posm_ref[0, :, pl.ds(t0, chunk)]
        xs_scr[...] += jnp.dot(jnp.where(hit, 1.0, 0.0).astype(BF16), h_ref[0, pl.ds(t0, chunk), :],
                               preferred_element_type=F32)
        gate_scr[...] += jnp.sum(jnp.where(hit, aff_ref[0, :, pl.ds(t0, chunk)], 0.0), axis=1, keepdims=True)
        return carry

    lax.fori_loop(lo_ref[lin], hi_ref[lin] + 1, body, 0)
    x = xs_scr[...].astype(BF16)
    a = jnp.dot(x, wg_ref[0], preferred_element_type=F32)
    u = jnp.dot(x, wu_ref[0], preferred_element_type=F32)
    act = (a * jax.nn.sigmoid(a) * u).astype(BF16)
    y = jnp.dot(act, wd_ref[0], preferred_element_type=F32)
    ys_ref[0, 0] = (y * gate_scr[...]).astype(BF16)


def _ffn(lo, hi, posm, aff, h2, wg, wu, wd, cap, rt, chunk):
    b, n, d = h2.shape
    e, _, hid = wg.shape
    nt = cap // rt
    rowmap = lambda bi, ei, ti, lo_r, hi_r: (bi * e + ei, 0, 0)
    grid_spec = pltpu.PrefetchScalarGridSpec(
        num_scalar_prefetch=2,
        grid=(b, e, nt),
        in_specs=[
            pl.BlockSpec((1, 1, n), rowmap),
            pl.BlockSpec((1, 1, n), rowmap),
            pl.BlockSpec((1, n, d), lambda bi, ei, ti, lo_r, hi_r: (bi, 0, 0), pipeline_mode=pl.Buffered(1)),
            pl.BlockSpec((1, d, hid), lambda bi, ei, ti, lo_r, hi_r: (ei, 0, 0)),
            pl.BlockSpec((1, d, hid), lambda bi, ei, ti, lo_r, hi_r: (ei, 0, 0)),
            pl.BlockSpec((1, hid, d), lambda bi, ei, ti, lo_r, hi_r: (ei, 0, 0)),
        ],
        out_specs=pl.BlockSpec((1, 1, rt, d), lambda bi, ei, ti, lo_r, hi_r: (bi, ei, ti, 0)),
        scratch_shapes=[pltpu.VMEM((rt, d), F32), pltpu.VMEM((rt, 1), F32)],
    )
    return pl.pallas_call(
        functools.partial(_ffn_kernel, rt=rt, chunk=chunk, ne=e, nt=nt),
        grid_spec=grid_spec,
        out_shape=jax.ShapeDtypeStruct((b, e, cap, d), BF16),
        compiler_params=_cparams(("arbitrary", "arbitrary", "arbitrary")),
        name="ffn",
    )(lo, hi, posm.reshape(b * e, 1, n), aff.reshape(b * e, 1, n), h2, wg, wu, wd)


def _combine_kernel(*refs, ne, ntt, wm, wt):
    p0_ref, tail_ref, post_ref = refs[0:3]
    ys_refs = refs[3:3 + ne]
    x_ref, gt2_ref, o_ref, acc_scr = refs[3 + ne:]
    bi, ti = pl.program_id(0), pl.program_id(1)
    base = (bi * ntt + ti) * ne
    tt = x_ref.shape[1]
    pt = post_ref[0]
    col = lax.broadcasted_iota(I32, (tt, wm), 1).astype(F32)

    def rel(e):
        return pt[:, e:e + 1] - p0_ref[base + e].astype(F32)

    total = None
    for e in range(ne):
        hit = jnp.where(rel(e) == col, 1.0, 0.0).astype(BF16)
        part = jnp.dot(hit, ys_refs[e][0, 0, 0:wm, :], preferred_element_type=F32)
        total = part if total is None else total + part
    acc_scr[...] = total

    if wt:
        col2 = (wm + lax.broadcasted_iota(I32, (tt, wt), 1)).astype(F32)
        for e in range(ne):
            @pl.when(tail_ref[base + e] > 0)
            def _(e=e):
                hit2 = jnp.where(rel(e) == col2, 1.0, 0.0).astype(BF16)
                acc_scr[...] += jnp.dot(hit2, ys_refs[e][0, 0, wm:wm + wt, :], preferred_element_type=F32)

    o_ref[0] = x_ref[0] + gt2_ref[0] * acc_scr[...]


def _combine(p0, tail, post, ys, x1, gt2, tt, wm, wt):
    b, n, d = x1.shape
    ne = ys.shape[1]
    ntt = n // tt
    w = wm + wt

    def ys_spec(e):
        def ys_map(bi, ti, p0_r, tail_r):
            return (bi, e, pl.multiple_of(p0_r[(bi * ntt + ti) * ne + e], BF16_SUBLANES), 0)
        return pl.BlockSpec((pl.Element(1), pl.Element(1), pl.Element(w), pl.Element(d)), ys_map)

    tile = lambda bi, ti, p0_r, tail_r: (bi, ti, 0)
    grid_spec = pltpu.PrefetchScalarGridSpec(
        num_scalar_prefetch=2,
        grid=(b, ntt),
        in_specs=[pl.BlockSpec((1, tt, LANES), tile)] + [ys_spec(e) for e in range(ne)] + [
            pl.BlockSpec((1, tt, d), tile),
            pl.BlockSpec((1, 1, d), lambda bi, ti, p0_r, tail_r: (bi, 0, 0)),
        ],
        out_specs=pl.BlockSpec((1, tt, d), tile),
        scratch_shapes=[pltpu.VMEM((tt, d), F32)],
    )
    return pl.pallas_call(
        functools.partial(_combine_kernel, ne=ne, ntt=ntt, wm=wm, wt=wt),
        grid_spec=grid_spec,
        out_shape=jax.ShapeDtypeStruct((b, n, d), F32),
        compiler_params=_cparams(("arbitrary", "arbitrary")),
        name="combine",
    )(p0, tail, post, *([ys] * ne), x1, gt2)


def _moe(h2, aff, x1, gt2, wg, wu, wd):
    b, n, d = x1.shape
    e = aff.shape[1]
    cap = CAPACITY_FACTOR * n // e
    chunk = MXU_DIM
    rt = min(MXU_DIM, cap)
    gchunk = min(2 * MXU_DIM, n)
    assert n % chunk == 0 and n % gchunk == 0 and cap % rt == 0
    posm, post, meta = _route(aff, cap, rt, chunk, gchunk)
    nt, ntt = cap // rt, n // chunk
    lo = meta[:, :, META_LO:META_LO + nt].astype(I32).reshape(-1)
    hi = meta[:, :, META_HI:META_HI + nt].astype(I32).reshape(-1)
    ys = _ffn(lo, hi, posm, aff, h2, wg, wu, wd, cap, rt, gchunk)

    wm = min(MXU_DIM, cap)
    wt = BF16_SUBLANES if cap >= MXU_DIM + BF16_SUBLANES else 0
    assert cap <= wm or wt > 0
    p = meta[:, :, META_P:META_P + ntt].astype(I32)
    p_next = jnp.concatenate([p[:, :, 1:], jnp.full((b, e, 1), cap, I32)], axis=2)
    p0 = jnp.clip((p // BF16_SUBLANES) * BF16_SUBLANES, 0, cap - (wm + wt))
    tail = (p_next > p0 + wm).astype(I32)
    p0 = jnp.transpose(p0, (0, 2, 1)).reshape(-1)
    tail = jnp.transpose(tail, (0, 2, 1)).reshape(-1)
    return _combine(p0, tail, post, ys, x1, gt2, chunk, wm, wt)


def _rope_tables(n):
    rows = n // GRID_W
    row = jnp.repeat(jnp.arange(rows, dtype=F32), GRID_W)
    col = jnp.tile(jnp.arange(GRID_W, dtype=F32), rows)
    n_freq = HEAD_DIM // 4
    inv = ROPE_BASE ** (-jnp.arange(n_freq, dtype=F32) / n_freq)
    ang = jnp.concatenate([row[:, None] * inv, col[:, None] * inv], axis=-1)
    cos, sin = jnp.cos(ang), jnp.sin(ang)
    cs = jnp.concatenate([cos, cos, cos, cos], axis=1)
    sn = jnp.concatenate([-sin, sin, -sin, sin], axis=1)
    return cs, sn


def kernel(x, c, ctx, c_ctx, w_ada, b_ada, g_mix, g_ffn, w_in, qg_a, kg_a, qg_b, kg_b, sink_b, conv_w, w_branch,
           w_out, w_router, w_e_gate, w_e_up, w_e_down):
    b, n, d = x.shape
    l_ctx = ctx.shape[1]
    depth = w_ada.shape[0]
    tm = 256
    tq = 256
    tk = next(t for t in (768, 512, 256, 128) if (n + l_ctx) % t == 0)
    assert n >= tq + 2 * WINDOW and l_ctx % tq == 0

    rows = -(-(b + 1) // 8) * 8
    s = jnp.concatenate([c, c_ctx[None, :], jnp.zeros((rows - b - 1, d), F32)], axis=0)
    mod_all = _ada(s, w_ada, b_ada)

    cs, sn = _rope_tables(n)
    cs_c = jnp.ones((l_ctx, LANES), F32)
    sn_c = jnp.zeros((l_ctx, LANES), F32)
    log2e = 1.4426950408889634
    scale = HEAD_DIM ** -0.5 * log2e

    xc = ctx
    for l in range(depth):
        last = l == depth - 1
        mod = mod_all[l]
        sh1, sc1, gt1, sh2, sc2, gt2 = [m[:, None, :] for m in jnp.split(mod[:b], 6, axis=-1)]
        shc1, scc1, gtc1, shc2, scc2, gtc2 = [jnp.broadcast_to(m[None, None, :], (b, 1, d))
                                              for m in jnp.split(mod[b], 6)]
        w_bf = w_in[l].astype(BF16)
        gq = (jnp.concatenate([jnp.tile(qg_a[l], N_HEADS), jnp.tile(qg_b[l], N_HEADS)]) * scale)[None, :]
        gk = jnp.concatenate([jnp.tile(kg_a[l], N_KV), jnp.tile(kg_b[l], N_KV)])[None, :]
        gm = g_mix[l][None, :]
        gf = g_ffn[l][None, :]
        wb_bf = w_branch[l].astype(BF16)
        wo_bf = w_out[l].astype(BF16)
        wr_t = jnp.transpose(w_router[l])
        wg, wu, wd = w_e_gate[l].astype(BF16), w_e_up[l].astype(BF16), w_e_down[l].astype(BF16)
        sink = sink_b[l].reshape(N_KV, GROUP) * log2e

        def sink_row(t):
            return jnp.repeat(sink, t, axis=1)[:, None, :]

        qa_c, qb_c, kta_c, va_c, ktb_c, vb_c, bg_c, u_c, gts_c = _inproj(
            xc, shc1, scc1, gm, w_bf, gq, gk, cs_c, sn_c, min(tm, l_ctx))
        qa, qb, kta, va, ktb, vb, bg, u, gts = _inproj(x, sh1, sc1, gm, w_bf, gq, gk, cs, sn, tm)

        k_all = jnp.concatenate([kta, kta_c], axis=1)
        vt_all = jnp.concatenate([va, va_c], axis=3)
        o_a = _attention(qa, k_all, vt_all, None, None, None, "global", tq, tk)
        o_b = _attention(qb, ktb, vb, ktb_c, vb_c, sink_row(tq), "window", tq, tk)
        x1, h2, aff = _merge(o_a, o_b, bg, u, gts, x, gt1, conv_w[l], wb_bf, wo_bf, gf, sh2, sc2, wr_t, tm)
        x = _moe(h2, aff, x1, gt2, wg, wu, wd)

        if not last:
            tqc = min(tq, l_ctx)
            oc_a = _attention(qa_c, None, None, kta_c, va_c, None, "none", tqc, tk)
            oc_b = _attention(qb_c, None, None, ktb_c, vb_c, sink_row(tqc), "none", tqc, tk)
            xc1, h2c, affc = _merge(oc_a, oc_b, bg_c, u_c, gts_c, xc, gtc1, conv_w[l], wb_bf, wo_bf, gf,
                                    shc2, scc2, wr_t, min(tm, l_ctx))
            xc = _moe(h2c, affc, xc1, gtc2, wg, wu, wd)
    return x
```

```python
import functools

import jax
import jax.numpy as jnp
from jax import lax
from jax.experimental import pallas as pl
from jax.experimental.pallas import tpu as pltpu

F32 = jnp.float32
BF16 = jnp.bfloat16
I32 = jnp.int32

HEAD_DIM = 64
N_HEADS = 8
N_KV = 2
GROUP = N_HEADS // N_KV
BRANCH = 512
N_EXPERTS = 16
CAPACITY_FACTOR = 2
GRID_W = 64
WINDOW = 128
ROPE_BASE = 10000.0
EPS = 1e-6
NEG_INF = -1e30

KV_W = N_KV * HEAD_DIM
OFF_QA = 4 * KV_W
OFF_QB = OFF_QA + N_HEADS * HEAD_DIM
OFF_CONV = OFF_QB + N_HEADS * HEAD_DIM
OFF_GATE = OFF_CONV + 3 * BRANCH

LANES = 128
V_ROWS = 80
MXU_DIM = 256
BF16_SUBLANES = 16
VMEM_LIMIT = 56 * 1024 * 1024


def _cparams(sem):
    return pltpu.CompilerParams(dimension_semantics=sem, vmem_limit_bytes=VMEM_LIMIT)


def _ada_kernel(s_ref, w_ref, b_ref, o_ref):
    s = s_ref[...]
    s = s * jax.nn.sigmoid(s)
    o_ref[0] = jnp.dot(s, w_ref[0], preferred_element_type=F32,
                       precision=lax.Precision.HIGHEST) + b_ref[0]


def _ada(s, w_ada, b_ada):
    depth, d, d6 = w_ada.shape
    rows = s.shape[0]
    tn = 1536
    return pl.pallas_call(
        _ada_kernel,
        grid=(depth, d6 // tn),
        in_specs=[
            pl.BlockSpec((rows, d), lambda l, j: (0, 0)),
            pl.BlockSpec((1, d, tn), lambda l, j: (l, 0, j)),
            pl.BlockSpec((1, 1, tn), lambda l, j: (l, 0, j)),
        ],
        out_specs=pl.BlockSpec((1, rows, tn), lambda l, j: (l, 0, j)),
        out_shape=jax.ShapeDtypeStruct((depth, rows, d6), F32),
        compiler_params=_cparams(("arbitrary", "arbitrary")),
        name="ada",
    )(s, w_ada, b_ada.reshape(depth, 1, d6))


def _inproj_kernel(x_ref, sh_ref, sc_ref, g_ref, w_ref, gq_ref, gk_ref, cs_ref, sn_ref,
                   qta_ref, qtb_ref, ka_ref, vta_ref, kb_ref, vtb_ref, bg_ref, u_ref, gt_ref):
    x = x_ref[0]
    tm = x.shape[0]
    ms = jnp.mean(x * x, axis=-1, keepdims=True)
    h = (x * lax.rsqrt(ms + EPS)) * g_ref[...]
    h = h * (1.0 + sc_ref[0]) + sh_ref[0]
    hb = h.astype(BF16)

    def proj(a, b):
        return jnp.dot(hb, w_ref[:, a:b], preferred_element_type=F32)

    slab = 2 * LANES
    r = lax.broadcasted_iota(I32, (slab, slab), 0) // HEAD_DIM
    c = lax.broadcasted_iota(I32, (slab, slab), 1) // HEAD_DIM
    head_mean = jnp.where(r == c, 1.0 / HEAD_DIM, 0.0).astype(BF16)
    cs = jnp.concatenate([cs_ref[...], cs_ref[...]], axis=1)
    sn = jnp.concatenate([sn_ref[...], sn_ref[...]], axis=1)
    lane = lax.broadcasted_iota(I32, (tm, slab), 1)
    first_half = (lane % HEAD_DIM) < (HEAD_DIM // 2)

    def head_norm_rope(p, gain):
        msq = jnp.dot((p * p).astype(BF16), head_mean, preferred_element_type=F32)
        y = p * lax.rsqrt(msq + EPS) * gain
        swapped = jnp.where(first_half, pltpu.roll(y, slab - HEAD_DIM // 2, 1), pltpu.roll(y, HEAD_DIM // 2, 1))
        return y * cs + swapped * sn

    pkv = proj(0, OFF_QA)
    k = jnp.concatenate([pkv[:, 0:KV_W], pkv[:, 2 * KV_W:3 * KV_W]], axis=1)
    k = head_norm_rope(k, gk_ref[...]).astype(BF16)
    ka_ref[0] = k[:, 0:KV_W]
    kb_ref[0] = k[:, KV_W:2 * KV_W]

    vt = jnp.transpose(jnp.concatenate([pkv[:, KV_W:2 * KV_W], pkv[:, 3 * KV_W:4 * KV_W]], axis=1))
    sub = lax.broadcasted_iota(I32, (V_ROWS - HEAD_DIM, tm), 0)
    ones_rows = jnp.where(sub == 0, 1.0, 0.0)
    for hh, ref in enumerate((vta_ref, vta_ref, vtb_ref, vtb_ref)):
        ref[0, hh % N_KV] = jnp.concatenate([vt[hh * HEAD_DIM:(hh + 1) * HEAD_DIM], ones_rows], axis=0).astype(BF16)

    pq = proj(OFF_QA, OFF_CONV)
    for s in range(4):
        q = head_norm_rope(pq[:, s * slab:(s + 1) * slab], gq_ref[:, s * slab:(s + 1) * slab])
        qt = jnp.transpose(q).astype(BF16)
        ref = qta_ref if s < 2 else qtb_ref
        for hh in range(GROUP):
            ref[0, (s % 2) * GROUP + hh] = qt[hh * HEAD_DIM:(hh + 1) * HEAD_DIM]

    pc = proj(OFF_CONV, OFF_GATE)
    bg_ref[0] = pc[:, 0:BRANCH].astype(BF16)
    u_ref[0] = (pc[:, BRANCH:2 * BRANCH] * pc[:, 2 * BRANCH:3 * BRANCH]).astype(BF16)

    d = x.shape[1]
    for j in range(3):
        pg = proj(OFF_GATE + j * d, OFF_GATE + (j + 1) * d)
        gt_ref[0, :, j * d:(j + 1) * d] = jax.nn.sigmoid(pg).astype(BF16)


def _inproj(x, sh, sc, g, w_bf, gq, gk, cs, sn, tm):
    b, n, d = x.shape
    pw = w_bf.shape[1]
    nq = N_HEADS * HEAD_DIM
    bmap = lambda bi, i: (bi, i, 0)
    qt_shape = jax.ShapeDtypeStruct((b, N_HEADS, HEAD_DIM, n), BF16)
    k_shape = jax.ShapeDtypeStruct((b, n, KV_W), BF16)
    vt_shape = jax.ShapeDtypeStruct((b, N_KV, V_ROWS, n), BF16)
    out_shape = [
        qt_shape, qt_shape, k_shape, vt_shape, k_shape, vt_shape,
        jax.ShapeDtypeStruct((b, n, BRANCH), BF16), jax.ShapeDtypeStruct((b, n, BRANCH), BF16),
        jax.ShapeDtypeStruct((b, n, 3 * d), BF16),
    ]
    qt_spec = pl.BlockSpec((1, N_HEADS, HEAD_DIM, tm), lambda bi, i: (bi, 0, 0, i))
    k_spec = pl.BlockSpec((1, tm, KV_W), bmap)
    vt_spec = pl.BlockSpec((1, N_KV, V_ROWS, tm), lambda bi, i: (bi, 0, 0, i))
    out_specs = [
        qt_spec, qt_spec, k_spec, vt_spec, k_spec, vt_spec,
        pl.BlockSpec((1, tm, BRANCH), bmap), pl.BlockSpec((1, tm, BRANCH), bmap),
        pl.BlockSpec((1, tm, 3 * d), bmap),
    ]
    in_specs = [
        pl.BlockSpec((1, tm, d), bmap),
        pl.BlockSpec((1, 1, d), lambda bi, i: (bi, 0, 0)),
        pl.BlockSpec((1, 1, d), lambda bi, i: (bi, 0, 0)),
        pl.BlockSpec((1, d), lambda bi, i: (0, 0)),
        pl.BlockSpec((d, pw), lambda bi, i: (0, 0), pipeline_mode=pl.Buffered(1)),
        pl.BlockSpec((1, 2 * nq), lambda bi, i: (0, 0)),
        pl.BlockSpec((1, 2 * LANES), lambda bi, i: (0, 0)),
        pl.BlockSpec((tm, LANES), lambda bi, i: (i, 0)),
        pl.BlockSpec((tm, LANES), lambda bi, i: (i, 0)),
    ]
    return pl.pallas_call(
        _inproj_kernel,
        grid=(b, n // tm),
        in_specs=in_specs,
        out_specs=out_specs,
        out_shape=out_shape,
        compiler_params=_cparams(("parallel", "parallel")),
        name="inproj",
    )(x, sh, sc, g, w_bf, gq, gk, cs, sn)


def _attn_kernel(*refs, mode, has_sink, tq, tk, n_lat):
    it = iter(refs)
    qt_ref = next(it)
    if mode != "none":
        kl_ref = next(it)
        vtl_ref = next(it)
    if mode != "global":
        kc_ref = next(it)
        vtc_ref = next(it)
    sink_ref = next(it) if has_sink else None
    o_ref = next(it)
    qp_scr = next(it)
    m_scr = next(it)
    acc_scr = next(it)
    if mode == "global":
        s_scr = next(it)
        cmax_scr = next(it)

    kv = pl.program_id(1)
    i = pl.program_id(2)
    for g in range(GROUP):
        qt = qt_ref[0, g]
        zero = jnp.zeros_like(qt)
        qp_scr[0:HEAD_DIM, g * tq:(g + 1) * tq] = jnp.where(kv == 0, qt, zero)
        qp_scr[HEAD_DIM:2 * HEAD_DIM, g * tq:(g + 1) * tq] = jnp.where(kv == 0, zero, qt)
    if has_sink:
        m_scr[...] = sink_ref[0]
    else:
        m_scr[...] = jnp.full(m_scr.shape, NEG_INF, F32)
    acc_scr[...] = jnp.zeros(acc_scr.shape, F32)

    def lanes_of(g):
        return slice(g * tq, (g + 1) * tq)

    def scores(g, k):
        return jnp.dot(k, qp_scr[:, lanes_of(g)], preferred_element_type=F32)

    def softmax(g, s, cmax):
        sl = lanes_of(g)
        m_prev = m_scr[:, sl]
        m_new = jnp.maximum(m_prev, cmax)
        m_scr[:, sl] = m_new
        return jnp.exp2(m_prev - m_new), jnp.exp2(s - m_new).astype(BF16)

    def weighted_values(g, alpha, p, vt):
        sl = lanes_of(g)
        acc_scr[:, sl] = alpha * acc_scr[:, sl] + jnp.dot(vt, p, preferred_element_type=F32)

    def softmax_pv(g, s, cmax, vt):
        alpha, p = softmax(g, s, cmax)
        weighted_values(g, alpha, p, vt)

    if mode == "global":
        nt = n_lat // tk

        def keys(j):
            return kl_ref[0, pl.ds(pl.multiple_of(j * tk, tk), tk), :]

        def values_t(j):
            return vtl_ref[0, 0, :, pl.ds(pl.multiple_of(j * tk, tk), tk)]

        def issue_scores(g, k, par):
            s = scores(g, k)
            s_scr[par, g] = s
            cmax_scr[par, :, lanes_of(g)] = jnp.max(s, axis=0, keepdims=True)

        k0 = keys(0)
        for g in range(GROUP):
            issue_scores(g, k0, 0)

        def consume(j, par):
            vt = values_t(j)
            for g in range(GROUP):
                softmax_pv(g, s_scr[par, g], cmax_scr[par, :, lanes_of(g)], vt)

        def pipelined(j, par):
            k_next = keys(j + 1)
            vt = values_t(j)
            for g in range(GROUP):
                alpha, p = softmax(g, s_scr[par, g], cmax_scr[par, :, lanes_of(g)])
                issue_scores(g, k_next, 1 - par)
                weighted_values(g, alpha, p, vt)

        def body(jj, carry):
            pipelined(2 * jj, 0)
            pipelined(2 * jj + 1, 1)
            return carry

        lax.fori_loop(0, (nt - 1) // 2, body, 0)
        if (nt - 1) % 2:
            pipelined(nt - 2, 0)
        consume(nt - 1, (nt - 1) % 2)
    else:
        if mode == "window":
            span = tq + 2 * WINDOW
            start = pl.multiple_of(jnp.clip(i * tq - WINDOW, 0, n_lat - span), LANES)
            l_ctx = kc_ref.shape[1]
            k = jnp.concatenate([kl_ref[0, pl.ds(start, span), :], kc_ref[0]], axis=0)
            vt = jnp.concatenate([vtl_ref[0, 0, :, pl.ds(start, span)], vtc_ref[0, 0]], axis=1)
            row = lax.broadcasted_iota(I32, (span + l_ctx, tq), 0)
            qpos = i * tq + lax.broadcasted_iota(I32, (span + l_ctx, tq), 1)
            mask = (row >= span) | (jnp.abs(qpos - (start + row)) <= WINDOW)
        else:
            k, vt, mask = kc_ref[0], vtc_ref[0, 0], None
        all_scores = [scores(g, k) for g in range(GROUP)]
        for g in range(GROUP):
            s = all_scores[g] if mask is None else jnp.where(mask, all_scores[g], NEG_INF)
            softmax_pv(g, s, jnp.max(s, axis=0, keepdims=True), vt)

    acc = acc_scr[...]
    denom = acc[HEAD_DIM:HEAD_DIM + 1, :]
    if has_sink:
        denom = denom + jnp.exp2(sink_ref[0] - m_scr[...])
    ot = acc / denom
    ot = jnp.concatenate([ot, jnp.zeros((LANES - V_ROWS, ot.shape[1]), F32)], axis=0)
    lane = lax.broadcasted_iota(I32, (tq, LANES), 1)
    heads = [jnp.transpose(ot[:, g * tq:(g + 1) * tq]) for g in range(GROUP)]
    pairs = [jnp.where(lane < HEAD_DIM, heads[2 * j], pltpu.roll(heads[2 * j + 1], HEAD_DIM, 1))
             for j in range(GROUP // 2)]
    o_ref[0] = jnp.concatenate(pairs, axis=1).astype(BF16)


def _attention(qt, k_lat, vt_lat, k_ctx, vt_ctx, sink, mode, tq, tk):
    b, _, _, nq = qt.shape
    n_lat = k_lat.shape[1] if mode != "none" else 0
    gw = GROUP * HEAD_DIM
    lanes = GROUP * tq
    kmap = lambda bi, kv, i: (bi, 0, 0)
    vmap = lambda bi, kv, i: (bi, kv, 0, 0)
    in_specs = [pl.BlockSpec((1, GROUP, HEAD_DIM, tq), lambda bi, kv, i: (bi, kv, 0, i))]
    args = [qt]
    if mode != "none":
        in_specs += [pl.BlockSpec((1, n_lat, KV_W), kmap), pl.BlockSpec((1, 1, V_ROWS, n_lat), vmap)]
        args += [k_lat, vt_lat]
    if mode != "global":
        l_ctx = k_ctx.shape[1]
        in_specs += [pl.BlockSpec((1, l_ctx, KV_W), kmap), pl.BlockSpec((1, 1, V_ROWS, l_ctx), vmap)]
        args += [k_ctx, vt_ctx]
    if sink is not None:
        in_specs.append(pl.BlockSpec((1, 1, lanes), lambda bi, kv, i: (kv, 0, 0)))
        args.append(sink)
    scratch = [pltpu.VMEM((2 * HEAD_DIM, lanes), BF16), pltpu.VMEM((1, lanes), F32), pltpu.VMEM((V_ROWS, lanes), F32)]
    if mode == "global":
        assert n_lat % tk == 0
        scratch += [pltpu.VMEM((2, GROUP, tk, tq), F32), pltpu.VMEM((2, 1, lanes), F32)]
    kern = functools.partial(_attn_kernel, mode=mode, has_sink=sink is not None, tq=tq, tk=tk, n_lat=n_lat)
    return pl.pallas_call(
        kern,
        grid=(b, N_KV, nq // tq),
        in_specs=in_specs,
        out_specs=pl.BlockSpec((1, tq, gw), lambda bi, kv, i: (bi, i, kv)),
        out_shape=jax.ShapeDtypeStruct((b, nq, N_HEADS * HEAD_DIM), BF16),
        scratch_shapes=scratch,
        compiler_params=_cparams(("parallel", "parallel", "parallel")),
        name="attn_" + mode + ("_sink" if sink is not None else ""),
    )(*args)


def _merge_kernel(oa_ref, ob_ref, bg_ref, u_ref, up_ref, un_ref, gts_ref, x_ref, gt1_ref, cw_ref, wb_ref, wo_ref,
                  gf_ref, sh2_ref, sc2_ref, wr_ref, x1_ref, h2_ref, aff_ref, *, nt):
    i = pl.program_id(1)
    tm, d = x_ref.shape[1], x_ref.shape[2]
    u = u_ref[0].astype(F32)
    prev_row = jnp.where(i > 0, up_ref[0][BF16_SUBLANES - 1:BF16_SUBLANES, :].astype(F32), 0.0)
    next_row = jnp.where(i < nt - 1, un_ref[0][0:1, :].astype(F32), 0.0)
    row = lax.broadcasted_iota(I32, (tm, BRANCH), 0)
    um1 = jnp.where(row == 0, prev_row, pltpu.roll(u, 1, 0))
    up1 = jnp.where(row == tm - 1, next_row, pltpu.roll(u, tm - 1, 0))
    cw = cw_ref[...]
    conv = cw[0:1] * um1 + cw[1:2] * u + cw[2:3] * up1
    oc = (bg_ref[0].astype(F32) * conv).astype(BF16)

    merged = gts_ref[0, :, 0:d].astype(F32) * jnp.dot(oa_ref[0], wb_ref[0], preferred_element_type=F32)
    merged += gts_ref[0, :, d:2 * d].astype(F32) * jnp.dot(ob_ref[0], wb_ref[1], preferred_element_type=F32)
    merged += gts_ref[0, :, 2 * d:3 * d].astype(F32) * jnp.dot(oc, wb_ref[2], preferred_element_type=F32)
    y = jnp.dot(merged.astype(BF16), wo_ref[...], preferred_element_type=F32)
    x1 = x_ref[0] + gt1_ref[0] * y
    x1_ref[0] = x1

    ms = jnp.mean(x1 * x1, axis=-1, keepdims=True)
    h2 = (x1 * lax.rsqrt(ms + EPS)) * gf_ref[...]
    h2 = h2 * (1.0 + sc2_ref[0]) + sh2_ref[0]
    h_hi = h2.astype(BF16)
    h2_ref[0] = h_hi

    h_lo = (h2 - h_hi.astype(F32)).astype(BF16)
    wr = wr_ref[...]
    wr_hi = wr.astype(BF16)
    wr_lo = (wr - wr_hi.astype(F32)).astype(BF16)
    dn = (((1,), (1,)), ((), ()))
    lg = lax.dot_general(wr_hi, h_hi, dn, preferred_element_type=F32)
    lg += lax.dot_general(wr_hi, h_lo, dn, preferred_element_type=F32)
    lg += lax.dot_general(wr_lo, h_hi, dn, preferred_element_type=F32)
    ex = jnp.exp(lg - jnp.max(lg, axis=0, keepdims=True))
    aff_ref[0] = ex / jnp.sum(ex, axis=0, keepdims=True)


def _merge(oa, ob, bg, u, gts, x, gt1, cw, wb_bf, wo_bf, gf, sh2, sc2, wr_t, tm):
    b, n, d = x.shape
    nt = n // tm
    e = wr_t.shape[0]
    bmap = lambda bi, i: (bi, i, 0)
    vec = lambda bi, i: (bi, 0, 0)
    halo = BF16_SUBLANES
    per = tm // halo
    last = n // halo - 1
    in_specs = [
        pl.BlockSpec((1, tm, BRANCH), bmap), pl.BlockSpec((1, tm, BRANCH), bmap),
        pl.BlockSpec((1, tm, BRANCH), bmap), pl.BlockSpec((1, tm, BRANCH), bmap),
        pl.BlockSpec((1, halo, BRANCH), lambda bi, i: (bi, jnp.maximum(i * per - 1, 0), 0)),
        pl.BlockSpec((1, halo, BRANCH), lambda bi, i: (bi, jnp.minimum((i + 1) * per, last), 0)),
        pl.BlockSpec((1, tm, 3 * d), bmap),
        pl.BlockSpec((1, tm, d), bmap),
        pl.BlockSpec((1, 1, d), vec),
        pl.BlockSpec((3, BRANCH), lambda bi, i: (0, 0)),
        pl.BlockSpec((3, BRANCH, d), lambda bi, i: (0, 0, 0)),
        pl.BlockSpec((d, d), lambda bi, i: (0, 0)),
        pl.BlockSpec((1, d), lambda bi, i: (0, 0)),
        pl.BlockSpec((1, 1, d), vec), pl.BlockSpec((1, 1, d), vec),
        pl.BlockSpec((e, d), lambda bi, i: (0, 0)),
    ]
    out_specs = [pl.BlockSpec((1, tm, d), bmap), pl.BlockSpec((1, tm, d), bmap),
                 pl.BlockSpec((1, e, tm), lambda bi, i: (bi, 0, i))]
    out_shape = [jax.ShapeDtypeStruct((b, n, d), F32), jax.ShapeDtypeStruct((b, n, d), BF16),
                 jax.ShapeDtypeStruct((b, e, n), F32)]
    return pl.pallas_call(
        functools.partial(_merge_kernel, nt=nt),
        grid=(b, nt),
        in_specs=in_specs,
        out_specs=out_specs,
        out_shape=out_shape,
        compiler_params=_cparams(("parallel", "parallel")),
        name="merge",
    )(oa, ob, bg, u, u, u, gts, x, gt1, cw, wb_bf, wo_bf, gf, sh2, sc2, wr_t)


META_P, META_LO, META_HI = 0, 64, 96


def _cumsum_lanes(x, chunk):
    r = lax.broadcasted_iota(I32, (chunk, chunk), 0)
    c = lax.broadcasted_iota(I32, (chunk, chunk), 1)
    upper = jnp.where(r <= c, 1.0, 0.0).astype(BF16)
    carry = jnp.zeros((x.shape[0], 1), F32)
    outs, starts = [], []
    for j in range(x.shape[1] // chunk):
        starts.append(carry)
        y = jnp.dot(x[:, j * chunk:(j + 1) * chunk].astype(BF16), upper, preferred_element_type=F32) + carry
        outs.append(y)
        carry = y[:, chunk - 1:chunk]
    return jnp.concatenate(outs, axis=1), starts


def _route_kernel(aff_ref, posm_ref, post_ref, meta_ref, *, cap, rt, chunk, gchunk):
    a = aff_ref[0]
    e, n = a.shape
    bits = pltpu.bitcast(a, I32)

    def search(it, cur):
        cand = cur | jnp.left_shift(jnp.int32(1), 30 - it)
        cnt = jnp.sum(jnp.where(bits >= cand, 1.0, 0.0), axis=1, keepdims=True)
        return jnp.where(cnt >= cap, cand, cur)

    tau = lax.fori_loop(0, 31, search, jnp.zeros((e, 1), I32))
    gt = bits > tau
    eq = bits == tau
    need = cap - jnp.sum(jnp.where(gt, 1.0, 0.0), axis=1, keepdims=True)
    eqf = jnp.where(eq, 1.0, 0.0)
    ceq, _ = _cumsum_lanes(eqf, chunk)
    sel = gt | (eq & ((ceq - eqf) < need))
    sf = jnp.where(sel, 1.0, 0.0)
    cin, starts = _cumsum_lanes(sf, chunk)
    posm = jnp.where(sel, cin - sf, -1.0)
    posm_ref[0] = posm
    pad = jnp.full((LANES - e, n), -1.0, F32)
    post_ref[0] = jnp.transpose(jnp.concatenate([posm, pad], axis=0))

    lane = lax.broadcasted_iota(I32, (e, LANES), 1)
    meta = jnp.zeros((e, LANES), F32)
    for t, st in enumerate(starts):
        meta = jnp.where(lane == META_P + t, st, meta)
    inv = 1.0 / gchunk
    for t in range(cap // rt):
        first = jnp.sum(jnp.where(cin <= float(rt * t), 1.0, 0.0), axis=1, keepdims=True)
        lastt = jnp.sum(jnp.where(cin < float(rt * (t + 1)), 1.0, 0.0), axis=1, keepdims=True)
        meta = jnp.where(lane == META_LO + t, jnp.floor(first * inv), meta)
        meta = jnp.where(lane == META_HI + t, jnp.floor(lastt * inv), meta)
    meta_ref[0] = meta


def _route(aff, cap, rt, chunk, gchunk):
    b, e, n = aff.shape
    assert n // chunk <= META_LO - META_P and cap // rt <= META_HI - META_LO
    return pl.pallas_call(
        functools.partial(_route_kernel, cap=cap, rt=rt, chunk=chunk, gchunk=gchunk),
        grid=(b,),
        in_specs=[pl.BlockSpec((1, e, n), lambda bi: (bi, 0, 0))],
        out_specs=[pl.BlockSpec((1, e, n), lambda bi: (bi, 0, 0)),
                   pl.BlockSpec((1, n, LANES), lambda bi: (bi, 0, 0)),
                   pl.BlockSpec((1, e, LANES), lambda bi: (bi, 0, 0))],
        out_shape=[jax.ShapeDtypeStruct((b, e, n), F32), jax.ShapeDtypeStruct((b, n, LANES), F32),
                   jax.ShapeDtypeStruct((b, e, LANES), F32)],
        compiler_params=_cparams(("parallel",)),
        name="route",
    )(aff)


def _ffn_kernel(lo_ref, hi_ref, posm_ref, aff_ref, h_ref, wg_ref, wu_ref, wd_ref, ys_ref, xs_scr, gate_scr,
                *, rt, chunk, ne, nt, expert_outer):
    ti = pl.program_id(2)
    bi, ei = (pl.program_id(1), pl.program_id(0)) if expert_outer else (pl.program_id(0), pl.program_id(1))
    lin = (bi * ne + ei) * nt + ti
    slot = (ti * rt + lax.broadcasted_iota(I32, (rt, 1), 0)).astype(F32)
    xs_scr[...] = jnp.zeros_like(xs_scr)
    gate_scr[...] = jnp.zeros_like(gate_scr)

    def body(c, carry):
        t0 = pl.multiple_of(c * chunk, chunk)
        hit = slot == posm_ref[0, :, pl.ds(t0, chunk)]
        xs_scr[...] += jnp.dot(jnp.where(hit, 1.0, 0.0).astype(BF16), h_ref[0, pl.ds(t0, chunk), :],
                               preferred_element_type=F32)
        gate_scr[...] += jnp.sum(jnp.where(hit, aff_ref[0, :, pl.ds(t0, chunk)], 0.0), axis=1, keepdims=True)
        return carry

    lax.fori_loop(lo_ref[lin], hi_ref[lin] + 1, body, 0)
    x = xs_scr[...].astype(BF16)
    a = jnp.dot(x, wg_ref[0], preferred_element_type=F32)
    u = jnp.dot(x, wu_ref[0], preferred_element_type=F32)
    act = (a * jax.nn.sigmoid(a) * u).astype(BF16)
    y = jnp.dot(act, wd_ref[0], preferred_element_type=F32)
    ys_ref[0, 0] = (y * gate_scr[...]).astype(BF16)


def _ffn(lo, hi, posm, aff, h2, wg, wu, wd, cap, rt, chunk):
    b, n, d = h2.shape
    e, _, hid = wg.shape
    nt = cap // rt
    expert_outer = n * d < 3 * d * hid

    def be(g0, g1):
        return (g1, g0) if expert_outer else (g0, g1)

    def rowmap(g0, g1, ti, lo_r, hi_r):
        bi, ei = be(g0, g1)
        return (bi * e + ei, 0, 0)

    def hmap(g0, g1, ti, lo_r, hi_r):
        return (be(g0, g1)[0], 0, 0)

    def wmap(g0, g1, ti, lo_r, hi_r):
        return (be(g0, g1)[1], 0, 0)

    def omap(g0, g1, ti, lo_r, hi_r):
        bi, ei = be(g0, g1)
        return (bi, ei, ti, 0)

    grid_spec = pltpu.PrefetchScalarGridSpec(
        num_scalar_prefetch=2,
        grid=(e, b, nt) if expert_outer else (b, e, nt),
        in_specs=[
            pl.BlockSpec((1, 1, n), rowmap),
            pl.BlockSpec((1, 1, n), rowmap),
            pl.BlockSpec((1, n, d), hmap, pipeline_mode=pl.Buffered(1)),
            pl.BlockSpec((1, d, hid), wmap),
            pl.BlockSpec((1, d, hid), wmap),
            pl.BlockSpec((1, hid, d), wmap),
        ],
        out_specs=pl.BlockSpec((1, 1, rt, d), omap),
        scratch_shapes=[pltpu.VMEM((rt, d), F32), pltpu.VMEM((rt, 1), F32)],
    )
    return pl.pallas_call(
        functools.partial(_ffn_kernel, rt=rt, chunk=chunk, ne=e, nt=nt, expert_outer=expert_outer),
        grid_spec=grid_spec,
        out_shape=jax.ShapeDtypeStruct((b, e, cap, d), BF16),
        compiler_params=_cparams(("arbitrary", "arbitrary", "arbitrary")),
        name="ffn",
    )(lo, hi, posm.reshape(b * e, 1, n), aff.reshape(b * e, 1, n), h2, wg, wu, wd)


def _combine_kernel(*refs, ne, ntt, wm, wt):
    p0_ref, tail_ref, post_ref = refs[0:3]
    ys_refs = refs[3:3 + ne]
    x_ref, gt2_ref, o_ref, acc_scr = refs[3 + ne:]
    bi, ti = pl.program_id(0), pl.program_id(1)
    base = (bi * ntt + ti) * ne
    tt = x_ref.shape[1]
    pt = post_ref[0]
    col = lax.broadcasted_iota(I32, (tt, wm), 1).astype(F32)

    def rel(e):
        return pt[:, e:e + 1] - p0_ref[base + e].astype(F32)

    total = None
    for e in range(ne):
        hit = jnp.where(rel(e) == col, 1.0, 0.0).astype(BF16)
        part = jnp.dot(hit, ys_refs[e][0, 0, 0:wm, :], preferred_element_type=F32)
        total = part if total is None else total + part
    acc_scr[...] = total

    if wt:
        col2 = (wm + lax.broadcasted_iota(I32, (tt, wt), 1)).astype(F32)
        for e in range(ne):
            @pl.when(tail_ref[base + e] > 0)
            def _(e=e):
                hit2 = jnp.where(rel(e) == col2, 1.0, 0.0).astype(BF16)
                acc_scr[...] += jnp.dot(hit2, ys_refs[e][0, 0, wm:wm + wt, :], preferred_element_type=F32)

    o_ref[0] = x_ref[0] + gt2_ref[0] * acc_scr[...]


def _combine(p0, tail, post, ys, x1, gt2, tt, wm, wt):
    b, n, d = x1.shape
    ne = ys.shape[1]
    ntt = n // tt
    w = wm + wt

    def ys_spec(e):
        def ys_map(bi, ti, p0_r, tail_r):
            return (bi, e, pl.multiple_of(p0_r[(bi * ntt + ti) * ne + e], BF16_SUBLANES), 0)
        return pl.BlockSpec((pl.Element(1), pl.Element(1), pl.Element(w), pl.Element(d)), ys_map)

    tile = lambda bi, ti, p0_r, tail_r: (bi, ti, 0)
    grid_spec = pltpu.PrefetchScalarGridSpec(
        num_scalar_prefetch=2,
        grid=(b, ntt),
        in_specs=[pl.BlockSpec((1, tt, LANES), tile)] + [ys_spec(e) for e in range(ne)] + [
            pl.BlockSpec((1, tt, d), tile),
            pl.BlockSpec((1, 1, d), lambda bi, ti, p0_r, tail_r: (bi, 0, 0)),
        ],
        out_specs=pl.BlockSpec((1, tt, d), tile),
        scratch_shapes=[pltpu.VMEM((tt, d), F32)],
    )
    return pl.pallas_call(
        functools.partial(_combine_kernel, ne=ne, ntt=ntt, wm=wm, wt=wt),
        grid_spec=grid_spec,
        out_shape=jax.ShapeDtypeStruct((b, n, d), F32),
        compiler_params=_cparams(("arbitrary", "arbitrary")),
        name="combine",
    )(p0, tail, post, *([ys] * ne), x1, gt2)


def _moe(h2, aff, x1, gt2, wg, wu, wd):
    b, n, d = x1.shape
    e = aff.shape[1]
    cap = CAPACITY_FACTOR * n // e
    chunk = MXU_DIM
    rt = min(MXU_DIM, cap)
    gchunk = min(2 * MXU_DIM, n)
    assert n % chunk == 0 and n % gchunk == 0 and cap % rt == 0
    posm, post, meta = _route(aff, cap, rt, chunk, gchunk)
    nt, ntt = cap // rt, n // chunk
    lo = meta[:, :, META_LO:META_LO + nt].astype(I32).reshape(-1)
    hi = meta[:, :, META_HI:META_HI + nt].astype(I32).reshape(-1)
    ys = _ffn(lo, hi, posm, aff, h2, wg, wu, wd, cap, rt, gchunk)

    wm = min(MXU_DIM, cap)
    wt = BF16_SUBLANES if cap >= MXU_DIM + BF16_SUBLANES else 0
    assert cap <= wm or wt > 0
    p = meta[:, :, META_P:META_P + ntt].astype(I32)
    p_next = jnp.concatenate([p[:, :, 1:], jnp.full((b, e, 1), cap, I32)], axis=2)
    p0 = jnp.clip((p // BF16_SUBLANES) * BF16_SUBLANES, 0, cap - (wm + wt))
    tail = (p_next > p0 + wm).astype(I32)
    p0 = jnp.transpose(p0, (0, 2, 1)).reshape(-1)
    tail = jnp.transpose(tail, (0, 2, 1)).reshape(-1)
    return _combine(p0, tail, post, ys, x1, gt2, chunk, wm, wt)


def _rope_tables(n):
    rows = n // GRID_W
    row = jnp.repeat(jnp.arange(rows, dtype=F32), GRID_W)
    col = jnp.tile(jnp.arange(GRID_W, dtype=F32), rows)
    n_freq = HEAD_DIM // 4
    inv = ROPE_BASE ** (-jnp.arange(n_freq, dtype=F32) / n_freq)
    ang = jnp.concatenate([row[:, None] * inv, col[:, None] * inv], axis=-1)
    cos, sin = jnp.cos(ang), jnp.sin(ang)
    cs = jnp.concatenate([cos, cos, cos, cos], axis=1)
    sn = jnp.concatenate([-sin, sin, -sin, sin], axis=1)
    return cs, sn


def kernel(x, c, ctx, c_ctx, w_ada, b_ada, g_mix, g_ffn, w_in, qg_a, kg_a, qg_b, kg_b, sink_b, conv_w, w_branch,
           w_out, w_router, w_e_gate, w_e_up, w_e_down):
    b, n, d = x.shape
    l_ctx = ctx.shape[1]
    depth = w_ada.shape[0]
    tm = 256
    tq = 256
    tk = next(t for t in (1408, 768, 512, 256, 128) if (n + l_ctx) % t == 0)
    assert n >= tq + 2 * WINDOW and l_ctx % tq == 0

    rows = -(-(b + 1) // 8) * 8
    s = jnp.concatenate([c, c_ctx[None, :], jnp.zeros((rows - b - 1, d), F32)], axis=0)
    mod_all = _ada(s, w_ada, b_ada)

    cs, sn = _rope_tables(n)
    cs_c = jnp.ones((l_ctx, LANES), F32)
    sn_c = jnp.zeros((l_ctx, LANES), F32)
    log2e = 1.4426950408889634
    scale = HEAD_DIM ** -0.5 * log2e

    xc = ctx
    for l in range(depth):
        last = l == depth - 1
        mod = mod_all[l]
        sh1, sc1, gt1, sh2, sc2, gt2 = [m[:, None, :] for m in jnp.split(mod[:b], 6, axis=-1)]
        shc1, scc1, gtc1, shc2, scc2, gtc2 = [jnp.broadcast_to(m[None, None, :], (b, 1, d))
                                              for m in jnp.split(mod[b], 6)]
        w_bf = w_in[l].astype(BF16)
        gq = (jnp.concatenate([jnp.tile(qg_a[l], N_HEADS), jnp.tile(qg_b[l], N_HEADS)]) * scale)[None, :]
        gk = jnp.concatenate([jnp.tile(kg_a[l], N_KV), jnp.tile(kg_b[l], N_KV)])[None, :]
        gm = g_mix[l][None, :]
        gf = g_ffn[l][None, :]
        wb_bf = w_branch[l].astype(BF16)
        wo_bf = w_out[l].astype(BF16)
        wr_t = jnp.transpose(w_router[l])
        wg, wu, wd = w_e_gate[l].astype(BF16), w_e_up[l].astype(BF16), w_e_down[l].astype(BF16)
        sink = sink_b[l].reshape(N_KV, GROUP) * log2e

        def sink_row(t):
            return jnp.repeat(sink, t, axis=1)[:, None, :]

        qa_c, qb_c, kta_c, va_c, ktb_c, vb_c, bg_c, u_c, gts_c = _inproj(
            xc, shc1, scc1, gm, w_bf, gq, gk, cs_c, sn_c, min(tm, l_ctx))
        qa, qb, kta, va, ktb, vb, bg, u, gts = _inproj(x, sh1, sc1, gm, w_bf, gq, gk, cs, sn, tm)

        k_all = jnp.concatenate([kta, kta_c], axis=1)
        vt_all = jnp.concatenate([va, va_c], axis=3)
        o_a = _attention(qa, k_all, vt_all, None, None, None, "global", tq, tk)
        o_b = _attention(qb, ktb, vb, ktb_c, vb_c, sink_row(tq), "window", tq, tk)
        x1, h2, aff = _merge(o_a, o_b, bg, u, gts, x, gt1, conv_w[l], wb_bf, wo_bf, gf, sh2, sc2, wr_t, tm)
        x = _moe(h2, aff, x1, gt2, wg, wu, wd)

        if not last:
            tqc = min(tq, l_ctx)
            oc_a = _attention(qa_c, None, None, kta_c, va_c, None, "none", tqc, tk)
            oc_b = _attention(qb_c, None, None, ktb_c, vb_c, sink_row(tqc), "none", tqc, tk)
            xc1, h2c, affc = _merge(oc_a, oc_b, bg_c, u_c, gts_c, xc, gtc1, conv_w[l], wb_bf, wo_bf, gf,
                                    shc2, scc2, wr_t, min(tm, l_ctx))
            xc = _moe(h2c, affc, xc1, gtc2, wg, wu, wd)
    return x
```

```python
import functools

import jax
import jax.numpy as jnp
from jax import lax
from jax.experimental import pallas as pl
from jax.experimental.pallas import tpu as pltpu

F32 = jnp.float32
BF16 = jnp.bfloat16
I32 = jnp.int32

HEAD_DIM = 64
N_HEADS = 8
N_KV = 2
GROUP = N_HEADS // N_KV
BRANCH = 512
N_EXPERTS = 16
CAPACITY_FACTOR = 2
GRID_W = 64
WINDOW = 128
ROPE_BASE = 10000.0
EPS = 1e-6
NEG_INF = -1e30

KV_W = N_KV * HEAD_DIM
OFF_QA = 4 * KV_W
OFF_QB = OFF_QA + N_HEADS * HEAD_DIM
OFF_CONV = OFF_QB + N_HEADS * HEAD_DIM
OFF_GATE = OFF_CONV + 3 * BRANCH

LANES = 128
V_ROWS = 80
MXU_DIM = 256
BF16_SUBLANES = 16
VMEM_LIMIT = 56 * 1024 * 1024


def _cparams(sem):
    return pltpu.CompilerParams(dimension_semantics=sem, vmem_limit_bytes=VMEM_LIMIT)


def _ada_kernel(s_ref, w_ref, b_ref, o_ref):
    s = s_ref[...]
    s = s * jax.nn.sigmoid(s)
    o_ref[0] = jnp.dot(s, w_ref[0], preferred_element_type=F32,
                       precision=lax.Precision.HIGHEST) + b_ref[0]


def _ada(s, w_ada, b_ada):
    depth, d, d6 = w_ada.shape
    rows = s.shape[0]
    tn = 1536
    return pl.pallas_call(
        _ada_kernel,
        grid=(depth, d6 // tn),
        in_specs=[
            pl.BlockSpec((rows, d), lambda l, j: (0, 0)),
            pl.BlockSpec((1, d, tn), lambda l, j: (l, 0, j)),
            pl.BlockSpec((1, 1, tn), lambda l, j: (l, 0, j)),
        ],
        out_specs=pl.BlockSpec((1, rows, tn), lambda l, j: (l, 0, j)),
        out_shape=jax.ShapeDtypeStruct((depth, rows, d6), F32),
        compiler_params=_cparams(("arbitrary", "arbitrary")),
        name="ada",
    )(s, w_ada, b_ada.reshape(depth, 1, d6))


def _inproj_kernel(x_ref, sh_ref, sc_ref, g_ref, w_ref, gq_ref, gk_ref, cs_ref, sn_ref,
                   qta_ref, qtb_ref, ka_ref, vta_ref, kb_ref, vtb_ref, bg_ref, u_ref, gt_ref):
    x = x_ref[0]
    tm = x.shape[0]
    ms = jnp.mean(x * x, axis=-1, keepdims=True)
    h = (x * lax.rsqrt(ms + EPS)) * g_ref[...]
    h = h * (1.0 + sc_ref[0]) + sh_ref[0]
    hb = h.astype(BF16)

    def proj(a, b):
        return jnp.dot(hb, w_ref[:, a:b], preferred_element_type=F32)

    slab = 2 * LANES
    r = lax.broadcasted_iota(I32, (slab, slab), 0) // HEAD_DIM
    c = lax.broadcasted_iota(I32, (slab, slab), 1) // HEAD_DIM
    head_mean = jnp.where(r == c, 1.0 / HEAD_DIM, 0.0).astype(BF16)
    cs = jnp.concatenate([cs_ref[...], cs_ref[...]], axis=1)
    sn = jnp.concatenate([sn_ref[...], sn_ref[...]], axis=1)
    lane = lax.broadcasted_iota(I32, (tm, slab), 1)
    first_half = (lane % HEAD_DIM) < (HEAD_DIM // 2)

    def head_norm_rope(p, gain):
        msq = jnp.dot((p * p).astype(BF16), head_mean, preferred_element_type=F32)
        y = p * lax.rsqrt(msq + EPS) * gain
        swapped = jnp.where(first_half, pltpu.roll(y, slab - HEAD_DIM // 2, 1), pltpu.roll(y, HEAD_DIM // 2, 1))
        return y * cs + swapped * sn

    pkv = proj(0, OFF_QA)
    k = jnp.concatenate([pkv[:, 0:KV_W], pkv[:, 2 * KV_W:3 * KV_W]], axis=1)
    k = head_norm_rope(k, gk_ref[...]).astype(BF16)
    ka_ref[0] = k[:, 0:KV_W]
    kb_ref[0] = k[:, KV_W:2 * KV_W]

    vt = jnp.transpose(jnp.concatenate([pkv[:, KV_W:2 * KV_W], pkv[:, 3 * KV_W:4 * KV_W]], axis=1))
    sub = lax.broadcasted_iota(I32, (V_ROWS - HEAD_DIM, tm), 0)
    ones_rows = jnp.where(sub == 0, 1.0, 0.0)
    for hh, ref in enumerate((vta_ref, vta_ref, vtb_ref, vtb_ref)):
        ref[0, hh % N_KV] = jnp.concatenate([vt[hh * HEAD_DIM:(hh + 1) * HEAD_DIM], ones_rows], axis=0).astype(BF16)

    pq = proj(OFF_QA, OFF_CONV)
    for s in range(4):
        q = head_norm_rope(pq[:, s * slab:(s + 1) * slab], gq_ref[:, s * slab:(s + 1) * slab])
        qt = jnp.transpose(q).astype(BF16)
        ref = qta_ref if s < 2 else qtb_ref
        for hh in range(GROUP):
            ref[0, (s % 2) * GROUP + hh] = qt[hh * HEAD_DIM:(hh + 1) * HEAD_DIM]

    pc = proj(OFF_CONV, OFF_GATE)
    bg_ref[0] = pc[:, 0:BRANCH].astype(BF16)
    u_ref[0] = (pc[:, BRANCH:2 * BRANCH] * pc[:, 2 * BRANCH:3 * BRANCH]).astype(BF16)

    d = x.shape[1]
    for j in range(3):
        pg = proj(OFF_GATE + j * d, OFF_GATE + (j + 1) * d)
        gt_ref[0, :, j * d:(j + 1) * d] = jax.nn.sigmoid(pg).astype(BF16)


def _inproj(x, sh, sc, g, w_bf, gq, gk, cs, sn, tm):
    b, n, d = x.shape
    pw = w_bf.shape[1]
    nq = N_HEADS * HEAD_DIM
    bmap = lambda bi, i: (bi, i, 0)
    qt_shape = jax.ShapeDtypeStruct((b, N_HEADS, HEAD_DIM, n), BF16)
    k_shape = jax.ShapeDtypeStruct((b, n, KV_W), BF16)
    vt_shape = jax.ShapeDtypeStruct((b, N_KV, V_ROWS, n), BF16)
    out_shape = [
        qt_shape, qt_shape, k_shape, vt_shape, k_shape, vt_shape,
        jax.ShapeDtypeStruct((b, n, BRANCH), BF16), jax.ShapeDtypeStruct((b, n, BRANCH), BF16),
        jax.ShapeDtypeStruct((b, n, 3 * d), BF16),
    ]
    qt_spec = pl.BlockSpec((1, N_HEADS, HEAD_DIM, tm), lambda bi, i: (bi, 0, 0, i))
    k_spec = pl.BlockSpec((1, tm, KV_W), bmap)
    vt_spec = pl.BlockSpec((1, N_KV, V_ROWS, tm), lambda bi, i: (bi, 0, 0, i))
    out_specs = [
        qt_spec, qt_spec, k_spec, vt_spec, k_spec, vt_spec,
        pl.BlockSpec((1, tm, BRANCH), bmap), pl.BlockSpec((1, tm, BRANCH), bmap),
        pl.BlockSpec((1, tm, 3 * d), bmap),
    ]
    in_specs = [
        pl.BlockSpec((1, tm, d), bmap),
        pl.BlockSpec((1, 1, d), lambda bi, i: (bi, 0, 0)),
        pl.BlockSpec((1, 1, d), lambda bi, i: (bi, 0, 0)),
        pl.BlockSpec((1, d), lambda bi, i: (0, 0)),
        pl.BlockSpec((d, pw), lambda bi, i: (0, 0), pipeline_mode=pl.Buffered(1)),
        pl.BlockSpec((1, 2 * nq), lambda bi, i: (0, 0)),
        pl.BlockSpec((1, 2 * LANES), lambda bi, i: (0, 0)),
        pl.BlockSpec((tm, LANES), lambda bi, i: (i, 0)),
        pl.BlockSpec((tm, LANES), lambda bi, i: (i, 0)),
    ]
    return pl.pallas_call(
        _inproj_kernel,
        grid=(b, n // tm),
        in_specs=in_specs,
        out_specs=out_specs,
        out_shape=out_shape,
        compiler_params=_cparams(("parallel", "parallel")),
        name="inproj",
    )(x, sh, sc, g, w_bf, gq, gk, cs, sn)


def _attn_kernel(*refs, mode, has_sink, tq, tk, n_lat):
    it = iter(refs)
    qt_ref = next(it)
    if mode != "none":
        kl_ref = next(it)
        vtl_ref = next(it)
    if mode != "global":
        kc_ref = next(it)
        vtc_ref = next(it)
    sink_ref = next(it) if has_sink else None
    o_ref = next(it)
    qp_scr = next(it)
    m_scr = next(it)
    acc_scr = next(it)
    if mode == "global":
        s_scr = next(it)
        cmax_scr = next(it)

    kv = pl.program_id(1)
    i = pl.program_id(2)
    for g in range(GROUP):
        qt = qt_ref[0, g]
        zero = jnp.zeros_like(qt)
        qp_scr[0:HEAD_DIM, g * tq:(g + 1) * tq] = jnp.where(kv == 0, qt, zero)
        qp_scr[HEAD_DIM:2 * HEAD_DIM, g * tq:(g + 1) * tq] = jnp.where(kv == 0, zero, qt)
    if has_sink:
        m_scr[...] = sink_ref[0]
    else:
        m_scr[...] = jnp.full(m_scr.shape, NEG_INF, F32)
    acc_scr[...] = jnp.zeros(acc_scr.shape, F32)

    def lanes_of(g):
        return slice(g * tq, (g + 1) * tq)

    def scores(g, k):
        return jnp.dot(k, qp_scr[:, lanes_of(g)], preferred_element_type=F32)

    def softmax(g, s, cmax):
        sl = lanes_of(g)
        m_prev = m_scr[:, sl]
        m_new = jnp.maximum(m_prev, cmax)
        m_scr[:, sl] = m_new
        return jnp.exp2(m_prev - m_new), jnp.exp2(s - m_new).astype(BF16)

    def weighted_values(g, alpha, p, vt):
        sl = lanes_of(g)
        acc_scr[:, sl] = alpha * acc_scr[:, sl] + jnp.dot(vt, p, preferred_element_type=F32)

    def softmax_pv(g, s, cmax, vt):
        alpha, p = softmax(g, s, cmax)
        weighted_values(g, alpha, p, vt)

    if mode == "global":
        nt = n_lat // tk

        def keys(j):
            return kl_ref[0, pl.ds(pl.multiple_of(j * tk, tk), tk), :]

        def values_t(j):
            return vtl_ref[0, 0, :, pl.ds(pl.multiple_of(j * tk, tk), tk)]

        def issue_scores(g, k, par):
            s = scores(g, k)
            s_scr[par, g] = s
            cmax_scr[par, :, lanes_of(g)] = jnp.max(s, axis=0, keepdims=True)

        k0 = keys(0)
        for g in range(GROUP):
            issue_scores(g, k0, 0)

        def consume(j, par):
            vt = values_t(j)
            for g in range(GROUP):
                softmax_pv(g, s_scr[par, g], cmax_scr[par, :, lanes_of(g)], vt)

        def pipelined(j, par):
            k_next = keys(j + 1)
            vt = values_t(j)
            for g in range(GROUP):
                alpha, p = softmax(g, s_scr[par, g], cmax_scr[par, :, lanes_of(g)])
                issue_scores(g, k_next, 1 - par)
                weighted_values(g, alpha, p, vt)

        def body(jj, carry):
            pipelined(2 * jj, 0)
            pipelined(2 * jj + 1, 1)
            return carry

        lax.fori_loop(0, (nt - 1) // 2, body, 0)
        if (nt - 1) % 2:
            pipelined(nt - 2, 0)
        consume(nt - 1, (nt - 1) % 2)
    else:
        if mode == "window":
            span = tq + 2 * WINDOW
            start = pl.multiple_of(jnp.clip(i * tq - WINDOW, 0, n_lat - span), LANES)
            l_ctx = kc_ref.shape[1]
            k = jnp.concatenate([kl_ref[0, pl.ds(start, span), :], kc_ref[0]], axis=0)
            vt = jnp.concatenate([vtl_ref[0, 0, :, pl.ds(start, span)], vtc_ref[0, 0]], axis=1)
            row = lax.broadcasted_iota(I32, (span + l_ctx, tq), 0)
            qpos = i * tq + lax.broadcasted_iota(I32, (span + l_ctx, tq), 1)
            mask = (row >= span) | (jnp.abs(qpos - (start + row)) <= WINDOW)
        else:
            k, vt, mask = kc_ref[0], vtc_ref[0, 0], None
        all_scores = [scores(g, k) for g in range(GROUP)]
        for g in range(GROUP):
            s = all_scores[g] if mask is None else jnp.where(mask, all_scores[g], NEG_INF)
            softmax_pv(g, s, jnp.max(s, axis=0, keepdims=True), vt)

    acc = acc_scr[...]
    denom = acc[HEAD_DIM:HEAD_DIM + 1, :]
    if has_sink:
        denom = denom + jnp.exp2(sink_ref[0] - m_scr[...])
    ot = acc / denom
    ot = jnp.concatenate([ot, jnp.zeros((LANES - V_ROWS, ot.shape[1]), F32)], axis=0)
    lane = lax.broadcasted_iota(I32, (tq, LANES), 1)
    heads = [jnp.transpose(ot[:, g * tq:(g + 1) * tq]) for g in range(GROUP)]
    pairs = [jnp.where(lane < HEAD_DIM, heads[2 * j], pltpu.roll(heads[2 * j + 1], HEAD_DIM, 1))
             for j in range(GROUP // 2)]
    o_ref[0] = jnp.concatenate(pairs, axis=1).astype(BF16)


def _attention(qt, k_lat, vt_lat, k_ctx, vt_ctx, sink, mode, tq, tk):
    b, _, _, nq = qt.shape
    n_lat = k_lat.shape[1] if mode != "none" else 0
    gw = GROUP * HEAD_DIM
    lanes = GROUP * tq
    kmap = lambda bi, kv, i: (bi, 0, 0)
    vmap = lambda bi, kv, i: (bi, kv, 0, 0)
    in_specs = [pl.BlockSpec((1, GROUP, HEAD_DIM, tq), lambda bi, kv, i: (bi, kv, 0, i))]
    args = [qt]
    if mode != "none":
        in_specs += [pl.BlockSpec((1, n_lat, KV_W), kmap), pl.BlockSpec((1, 1, V_ROWS, n_lat), vmap)]
        args += [k_lat, vt_lat]
    if mode != "global":
        l_ctx = k_ctx.shape[1]
        in_specs += [pl.BlockSpec((1, l_ctx, KV_W), kmap), pl.BlockSpec((1, 1, V_ROWS, l_ctx), vmap)]
        args += [k_ctx, vt_ctx]
    if sink is not None:
        in_specs.append(pl.BlockSpec((1, 1, lanes), lambda bi, kv, i: (kv, 0, 0)))
        args.append(sink)
    scratch = [pltpu.VMEM((2 * HEAD_DIM, lanes), BF16), pltpu.VMEM((1, lanes), F32), pltpu.VMEM((V_ROWS, lanes), F32)]
    if mode == "global":
        assert n_lat % tk == 0
        scratch += [pltpu.VMEM((2, GROUP, tk, tq), F32), pltpu.VMEM((2, 1, lanes), F32)]
    kern = functools.partial(_attn_kernel, mode=mode, has_sink=sink is not None, tq=tq, tk=tk, n_lat=n_lat)
    return pl.pallas_call(
        kern,
        grid=(b, N_KV, nq // tq),
        in_specs=in_specs,
        out_specs=pl.BlockSpec((1, tq, gw), lambda bi, kv, i: (bi, i, kv)),
        out_shape=jax.ShapeDtypeStruct((b, nq, N_HEADS * HEAD_DIM), BF16),
        scratch_shapes=scratch,
        compiler_params=_cparams(("parallel", "parallel", "parallel")),
        name="attn_" + mode + ("_sink" if sink is not None else ""),
    )(*args)


def _merge_kernel(oa_ref, ob_ref, bg_ref, u_ref, up_ref, un_ref, gts_ref, x_ref, gt1_ref, cw_ref, wb_ref, wo_ref,
                  gf_ref, sh2_ref, sc2_ref, wr_ref, x1_ref, h2_ref, aff_ref, *, nt):
    i = pl.program_id(1)
    tm, d = x_ref.shape[1], x_ref.shape[2]
    u = u_ref[0].astype(F32)
    prev_row = jnp.where(i > 0, up_ref[0][BF16_SUBLANES - 1:BF16_SUBLANES, :].astype(F32), 0.0)
    next_row = jnp.where(i < nt - 1, un_ref[0][0:1, :].astype(F32), 0.0)
    row = lax.broadcasted_iota(I32, (tm, BRANCH), 0)
    um1 = jnp.where(row == 0, prev_row, pltpu.roll(u, 1, 0))
    up1 = jnp.where(row == tm - 1, next_row, pltpu.roll(u, tm - 1, 0))
    cw = cw_ref[...]
    conv = cw[0:1] * um1 + cw[1:2] * u + cw[2:3] * up1
    oc = (bg_ref[0].astype(F32) * conv).astype(BF16)

    merged = gts_ref[0, :, 0:d].astype(F32) * jnp.dot(oa_ref[0], wb_ref[0], preferred_element_type=F32)
    merged += gts_ref[0, :, d:2 * d].astype(F32) * jnp.dot(ob_ref[0], wb_ref[1], preferred_element_type=F32)
    merged += gts_ref[0, :, 2 * d:3 * d].astype(F32) * jnp.dot(oc, wb_ref[2], preferred_element_type=F32)
    y = jnp.dot(merged.astype(BF16), wo_ref[...], preferred_element_type=F32)
    x1 = x_ref[0] + gt1_ref[0] * y
    x1_ref[0] = x1

    ms = jnp.mean(x1 * x1, axis=-1, keepdims=True)
    h2 = (x1 * lax.rsqrt(ms + EPS)) * gf_ref[...]
    h2 = h2 * (1.0 + sc2_ref[0]) + sh2_ref[0]
    h_hi = h2.astype(BF16)
    h2_ref[0] = h_hi

    h_lo = (h2 - h_hi.astype(F32)).astype(BF16)
    wr = wr_ref[...]
    wr_hi = wr.astype(BF16)
    wr_lo = (wr - wr_hi.astype(F32)).astype(BF16)
    dn = (((1,), (1,)), ((), ()))
    lg = lax.dot_general(wr_hi, h_hi, dn, preferred_element_type=F32)
    lg += lax.dot_general(wr_hi, h_lo, dn, preferred_element_type=F32)
    lg += lax.dot_general(wr_lo, h_hi, dn, preferred_element_type=F32)
    ex = jnp.exp(lg - jnp.max(lg, axis=0, keepdims=True))
    aff_ref[0] = ex / jnp.sum(ex, axis=0, keepdims=True)


def _merge(oa, ob, bg, u, gts, x, gt1, cw, wb_bf, wo_bf, gf, sh2, sc2, wr_t, tm):
    b, n, d = x.shape
    nt = n // tm
    e = wr_t.shape[0]
    bmap = lambda bi, i: (bi, i, 0)
    vec = lambda bi, i: (bi, 0, 0)
    halo = BF16_SUBLANES
    per = tm // halo
    last = n // halo - 1
    in_specs = [
        pl.BlockSpec((1, tm, BRANCH), bmap), pl.BlockSpec((1, tm, BRANCH), bmap),
        pl.BlockSpec((1, tm, BRANCH), bmap), pl.BlockSpec((1, tm, BRANCH), bmap),
        pl.BlockSpec((1, halo, BRANCH), lambda bi, i: (bi, jnp.maximum(i * per - 1, 0), 0)),
        pl.BlockSpec((1, halo, BRANCH), lambda bi, i: (bi, jnp.minimum((i + 1) * per, last), 0)),
        pl.BlockSpec((1, tm, 3 * d), bmap),
        pl.BlockSpec((1, tm, d), bmap),
        pl.BlockSpec((1, 1, d), vec),
        pl.BlockSpec((3, BRANCH), lambda bi, i: (0, 0)),
        pl.BlockSpec((3, BRANCH, d), lambda bi, i: (0, 0, 0)),
        pl.BlockSpec((d, d), lambda bi, i: (0, 0)),
        pl.BlockSpec((1, d), lambda bi, i: (0, 0)),
        pl.BlockSpec((1, 1, d), vec), pl.BlockSpec((1, 1, d), vec),
        pl.BlockSpec((e, d), lambda bi, i: (0, 0)),
    ]
    out_specs = [pl.BlockSpec((1, tm, d), bmap), pl.BlockSpec((1, tm, d), bmap),
                 pl.BlockSpec((1, e, tm), lambda bi, i: (bi, 0, i))]
    out_shape = [jax.ShapeDtypeStruct((b, n, d), F32), jax.ShapeDtypeStruct((b, n, d), BF16),
                 jax.ShapeDtypeStruct((b, e, n), F32)]
    return pl.pallas_call(
        functools.partial(_merge_kernel, nt=nt),
        grid=(b, nt),
        in_specs=in_specs,
        out_specs=out_specs,
        out_shape=out_shape,
        compiler_params=_cparams(("parallel", "parallel")),
        name="merge",
    )(oa, ob, bg, u, u, u, gts, x, gt1, cw, wb_bf, wo_bf, gf, sh2, sc2, wr_t)


META_P, META_LO, META_HI = 0, 64, 96


def _cumsum_lanes(x, chunk):
    r = lax.broadcasted_iota(I32, (chunk, chunk), 0)
    c = lax.broadcasted_iota(I32, (chunk, chunk), 1)
    upper = jnp.where(r <= c, 1.0, 0.0).astype(BF16)
    carry = jnp.zeros((x.shape[0], 1), F32)
    outs, starts = [], []
    for j in range(x.shape[1] // chunk):
        starts.append(carry)
        y = jnp.dot(x[:, j * chunk:(j + 1) * chunk].astype(BF16), upper, preferred_element_type=F32) + carry
        outs.append(y)
        carry = y[:, chunk - 1:chunk]
    return jnp.concatenate(outs, axis=1), starts


def _route_kernel(aff_ref, posm_ref, post_ref, meta_ref, *, cap, rt, chunk, gchunk):
    a = aff_ref[0]
    e, n = a.shape
    bits = pltpu.bitcast(a, I32)

    def search(it, cur):
        cand = cur | jnp.left_shift(jnp.int32(1), 30 - it)
        cnt = jnp.sum(jnp.where(bits >= cand, 1.0, 0.0), axis=1, keepdims=True)
        return jnp.where(cnt >= cap, cand, cur)

    tau = lax.fori_loop(0, 31, search, jnp.zeros((e, 1), I32))
    gt = bits > tau
    eq = bits == tau
    need = cap - jnp.sum(jnp.where(gt, 1.0, 0.0), axis=1, keepdims=True)
    eqf = jnp.where(eq, 1.0, 0.0)
    ceq, _ = _cumsum_lanes(eqf, chunk)
    sel = gt | (eq & ((ceq - eqf) < need))
    sf = jnp.where(sel, 1.0, 0.0)
    cin, starts = _cumsum_lanes(sf, chunk)
    posm = jnp.where(sel, cin - sf, -1.0)
    posm_ref[0] = posm
    pad = jnp.full((LANES - e, n), -1.0, F32)
    post_ref[0] = jnp.transpose(jnp.concatenate([posm, pad], axis=0))

    lane = lax.broadcasted_iota(I32, (e, LANES), 1)
    meta = jnp.zeros((e, LANES), F32)
    for t, st in enumerate(starts):
        meta = jnp.where(lane == META_P + t, st, meta)
    inv = 1.0 / gchunk
    for t in range(cap // rt):
        first = jnp.sum(jnp.where(cin <= float(rt * t), 1.0, 0.0), axis=1, keepdims=True)
        lastt = jnp.sum(jnp.where(cin < float(rt * (t + 1)), 1.0, 0.0), axis=1, keepdims=True)
        meta = jnp.where(lane == META_LO + t, jnp.floor(first * inv), meta)
        meta = jnp.where(lane == META_HI + t, jnp.floor(lastt * inv), meta)
    meta_ref[0] = meta


def _route(aff, cap, rt, chunk, gchunk):
    b, e, n = aff.shape
    assert n // chunk <= META_LO - META_P and cap // rt <= META_HI - META_LO
    return pl.pallas_call(
        functools.partial(_route_kernel, cap=cap, rt=rt, chunk=chunk, gchunk=gchunk),
        grid=(b,),
        in_specs=[pl.BlockSpec((1, e, n), lambda bi: (bi, 0, 0))],
        out_specs=[pl.BlockSpec((1, e, n), lambda bi: (bi, 0, 0)),
                   pl.BlockSpec((1, n, LANES), lambda bi: (bi, 0, 0)),
                   pl.BlockSpec((1, e, LANES), lambda bi: (bi, 0, 0))],
        out_shape=[jax.ShapeDtypeStruct((b, e, n), F32), jax.ShapeDtypeStruct((b, n, LANES), F32),
                   jax.ShapeDtypeStruct((b, e, LANES), F32)],
        compiler_params=_cparams(("parallel",)),
        name="route",
    )(aff)


def _ffn_kernel(lo_ref, hi_ref, posm_ref, aff_ref, h_ref, wg_ref, wu_ref, wd_ref, ys_ref, xs_scr, gate_scr,
                *, rt, chunk, ne, nt):
    bi, ei, ti = pl.program_id(0), pl.program_id(1), pl.program_id(2)
    lin = (bi * ne + ei) * nt + ti
    slot = (ti * rt + lax.broadcasted_iota(I32, (rt, 1), 0)).astype(F32)
    xs_scr[...] = jnp.zeros_like(xs_scr)
    gate_scr[...] = jnp.zeros_like(gate_scr)

    def body(c, carry):
        t0 = pl.multiple_of(c * chunk, chunk)
        hit = slot == posm_ref[0, :, pl.ds(t0, chunk)]
        xs_scr[...] += jnp.dot(jnp.where(hit, 1.0, 0.0).astype(BF16), h_ref[0, pl.ds(t0, chunk), :],
                               preferred_element_type=F32)
        gate_scr[...] += jnp.sum(jnp.where(hit, aff_ref[0, :, pl.ds(t0, chunk)], 0.0), axis=1, keepdims=True)
        return carry

    lax.fori_loop(lo_ref[lin], hi_ref[lin] + 1, body, 0)
    x = xs_scr[...].astype(BF16)
    a = jnp.dot(x, wg_ref[0], preferred_element_type=F32)
    u = jnp.dot(x, wu_ref[0], preferred_element_type=F32)
    act = (a * jax.nn.sigmoid(a) * u).astype(BF16)
    y = jnp.dot(act, wd_ref[0], preferred_element_type=F32)
    ys_ref[0, 0] = (y * gate_scr[...]).astype(BF16)


def _ffn(lo, hi, posm, aff, h2, wg, wu, wd, cap, rt, chunk):
    b, n, d = h2.shape
    e, _, hid = wg.shape
    nt = cap // rt
    rowmap = lambda bi, ei, ti, lo_r, hi_r: (bi * e + ei, 0, 0)
    wmap = lambda bi, ei, ti, lo_r, hi_r: (ei, 0, 0)
    grid_spec = pltpu.PrefetchScalarGridSpec(
        num_scalar_prefetch=2,
        grid=(b, e, nt),
        in_specs=[
            pl.BlockSpec((1, 1, n), rowmap),
            pl.BlockSpec((1, 1, n), rowmap),
            pl.BlockSpec((1, n, d), lambda bi, ei, ti, lo_r, hi_r: (bi, 0, 0), pipeline_mode=pl.Buffered(1)),
            pl.BlockSpec((1, d, hid), wmap),
            pl.BlockSpec((1, d, hid), wmap),
            pl.BlockSpec((1, hid, d), wmap),
        ],
        out_specs=pl.BlockSpec((1, 1, rt, d), lambda bi, ei, ti, lo_r, hi_r: (bi, ei, ti, 0)),
        scratch_shapes=[pltpu.VMEM((rt, d), F32), pltpu.VMEM((rt, 1), F32)],
    )
    return pl.pallas_call(
        functools.partial(_ffn_kernel, rt=rt, chunk=chunk, ne=e, nt=nt),
        grid_spec=grid_spec,
        out_shape=jax.ShapeDtypeStruct((b, e, cap, d), BF16),
        compiler_params=_cparams(("arbitrary", "arbitrary", "arbitrary")),
        name="ffn",
    )(lo, hi, posm.reshape(b * e, 1, n), aff.reshape(b * e, 1, n), h2, wg, wu, wd)


def _combine_kernel(*refs, ne, ntt, wm, wt):
    p0_ref, tail_ref, post_ref = refs[0:3]
    ys_refs = refs[3:3 + ne]
    x_ref, gt2_ref, o_ref, acc_scr = refs[3 + ne:]
    bi, ti = pl.program_id(0), pl.program_id(1)
    base = (bi * ntt + ti) * ne
    tt = x_ref.shape[1]
    pt = post_ref[0]
    col = lax.broadcasted_iota(I32, (tt, wm), 1).astype(F32)

    def rel(e):
        return pt[:, e:e + 1] - p0_ref[base + e].astype(F32)

    total = None
    for e in range(ne):
        hit = jnp.where(rel(e) == col, 1.0, 0.0).astype(BF16)
        part = jnp.dot(hit, ys_refs[e][0, 0, 0:wm, :], preferred_element_type=F32)
        total = part if total is None else total + part
    acc_scr[...] = total

    if wt:
        col2 = (wm + lax.broadcasted_iota(I32, (tt, wt), 1)).astype(F32)
        for e in range(ne):
            @pl.when(tail_ref[base + e] > 0)
            def _(e=e):
                hit2 = jnp.where(rel(e) == col2, 1.0, 0.0).astype(BF16)
                acc_scr[...] += jnp.dot(hit2, ys_refs[e][0, 0, wm:wm + wt, :], preferred_element_type=F32)

    o_ref[0] = x_ref[0] + gt2_ref[0] * acc_scr[...]


def _combine(p0, tail, post, ys, x1, gt2, tt, wm, wt):
    b, n, d = x1.shape
    ne = ys.shape[1]
    ntt = n // tt
    w = wm + wt

    def ys_spec(e):
        def ys_map(bi, ti, p0_r, tail_r):
            return (bi, e, pl.multiple_of(p0_r[(bi * ntt + ti) * ne + e], BF16_SUBLANES), 0)
        return pl.BlockSpec((pl.Element(1), pl.Element(1), pl.Element(w), pl.Element(d)), ys_map)

    tile = lambda bi, ti, p0_r, tail_r: (bi, ti, 0)
    grid_spec = pltpu.PrefetchScalarGridSpec(
        num_scalar_prefetch=2,
        grid=(b, ntt),
        in_specs=[pl.BlockSpec((1, tt, LANES), tile)] + [ys_spec(e) for e in range(ne)] + [
            pl.BlockSpec((1, tt, d), tile),
            pl.BlockSpec((1, 1, d), lambda bi, ti, p0_r, tail_r: (bi, 0, 0)),
        ],
        out_specs=pl.BlockSpec((1, tt, d), tile),
        scratch_shapes=[pltpu.VMEM((tt, d), F32)],
    )
    return pl.pallas_call(
        functools.partial(_combine_kernel, ne=ne, ntt=ntt, wm=wm, wt=wt),
        grid_spec=grid_spec,
        out_shape=jax.ShapeDtypeStruct((b, n, d), F32),
        compiler_params=_cparams(("arbitrary", "arbitrary")),
        name="combine",
    )(p0, tail, post, *([ys] * ne), x1, gt2)


def _moe(h2, aff, x1, gt2, wg, wu, wd):
    b, n, d = x1.shape
    e = aff.shape[1]
    cap = CAPACITY_FACTOR * n // e
    chunk = MXU_DIM
    rt = min(MXU_DIM, cap)
    gchunk = min(2 * MXU_DIM, n)
    assert n % chunk == 0 and n % gchunk == 0 and cap % rt == 0
    posm, post, meta = _route(aff, cap, rt, chunk, gchunk)
    nt, ntt = cap // rt, n // chunk
    lo = meta[:, :, META_LO:META_LO + nt].astype(I32).reshape(-1)
    hi = meta[:, :, META_HI:META_HI + nt].astype(I32).reshape(-1)
    ys = _ffn(lo, hi, posm, aff, h2, wg, wu, wd, cap, rt, gchunk)

    wm = min(MXU_DIM, cap)
    wt = BF16_SUBLANES if cap >= MXU_DIM + BF16_SUBLANES else 0
    assert cap <= wm or wt > 0
    p = meta[:, :, META_P:META_P + ntt].astype(I32)
    p_next = jnp.concatenate([p[:, :, 1:], jnp.full((b, e, 1), cap, I32)], axis=2)
    p0 = jnp.clip((p // BF16_SUBLANES) * BF16_SUBLANES, 0, cap - (wm + wt))
    tail = (p_next > p0 + wm).astype(I32)
    p0 = jnp.transpose(p0, (0, 2, 1)).reshape(-1)
    tail = jnp.transpose(tail, (0, 2, 1)).reshape(-1)
    return _combine(p0, tail, post, ys, x1, gt2, chunk, wm, wt)


def _rope_tables(n):
    rows = n // GRID_W
    row = jnp.repeat(jnp.arange(rows, dtype=F32), GRID_W)
    col = jnp.tile(jnp.arange(GRID_W, dtype=F32), rows)
    n_freq = HEAD_DIM // 4
    inv = ROPE_BASE ** (-jnp.arange(n_freq, dtype=F32) / n_freq)
    ang = jnp.concatenate([row[:, None] * inv, col[:, None] * inv], axis=-1)
    cos, sin = jnp.cos(ang), jnp.sin(ang)
    cs = jnp.concatenate([cos, cos, cos, cos], axis=1)
    sn = jnp.concatenate([-sin, sin, -sin, sin], axis=1)
    return cs, sn


def _tile_sizes(n, l_ctx):
    tm = MXU_DIM
    tm_in = min(2 * MXU_DIM, n)
    tq = MXU_DIM
    tk = next(t for t in (11 * LANES, 6 * LANES, 4 * LANES, 2 * LANES, LANES) if (n + l_ctx) % t == 0)
    assert n % tm_in == 0 and n >= tq + 2 * WINDOW and l_ctx % tq == 0
    return tm, tm_in, tq, tk


def kernel(x, c, ctx, c_ctx, w_ada, b_ada, g_mix, g_ffn, w_in, qg_a, kg_a, qg_b, kg_b, sink_b, conv_w, w_branch,
           w_out, w_router, w_e_gate, w_e_up, w_e_down):
    b, n, d = x.shape
    l_ctx = ctx.shape[1]
    depth = w_ada.shape[0]
    tm, tm_in, tq, tk = _tile_sizes(n, l_ctx)

    rows = -(-(b + 1) // 8) * 8
    s = jnp.concatenate([c, c_ctx[None, :], jnp.zeros((rows - b - 1, d), F32)], axis=0)
    mod_all = _ada(s, w_ada, b_ada)

    cs, sn = _rope_tables(n)
    cs_c = jnp.ones((l_ctx, LANES), F32)
    sn_c = jnp.zeros((l_ctx, LANES), F32)
    log2e = 1.4426950408889634
    scale = HEAD_DIM ** -0.5 * log2e

    xc = ctx
    for l in range(depth):
        last = l == depth - 1
        mod = mod_all[l]
        sh1, sc1, gt1, sh2, sc2, gt2 = [m[:, None, :] for m in jnp.split(mod[:b], 6, axis=-1)]
        shc1, scc1, gtc1, shc2, scc2, gtc2 = [jnp.broadcast_to(m[None, None, :], (b, 1, d))
                                              for m in jnp.split(mod[b], 6)]
        w_bf = w_in[l].astype(BF16)
        gq = (jnp.concatenate([jnp.tile(qg_a[l], N_HEADS), jnp.tile(qg_b[l], N_HEADS)]) * scale)[None, :]
        gk = jnp.concatenate([jnp.tile(kg_a[l], N_KV), jnp.tile(kg_b[l], N_KV)])[None, :]
        gm = g_mix[l][None, :]
        gf = g_ffn[l][None, :]
        wb_bf = w_branch[l].astype(BF16)
        wo_bf = w_out[l].astype(BF16)
        wr_t = jnp.transpose(w_router[l])
        wg, wu, wd = w_e_gate[l].astype(BF16), w_e_up[l].astype(BF16), w_e_down[l].astype(BF16)
        sink = sink_b[l].reshape(N_KV, GROUP) * log2e

        def sink_row(t):
            return jnp.repeat(sink, t, axis=1)[:, None, :]

        qa_c, qb_c, kta_c, va_c, ktb_c, vb_c, bg_c, u_c, gts_c = _inproj(
            xc, shc1, scc1, gm, w_bf, gq, gk, cs_c, sn_c, min(tm, l_ctx))
        qa, qb, kta, va, ktb, vb, bg, u, gts = _inproj(x, sh1, sc1, gm, w_bf, gq, gk, cs, sn, tm_in)

        k_all = jnp.concatenate([kta, kta_c], axis=1)
        vt_all = jnp.concatenate([va, va_c], axis=3)
        o_a = _attention(qa, k_all, vt_all, None, None, None, "global", tq, tk)
        o_b = _attention(qb, ktb, vb, ktb_c, vb_c, sink_row(tq), "window", tq, tk)
        x1, h2, aff = _merge(o_a, o_b, bg, u, gts, x, gt1, conv_w[l], wb_bf, wo_bf, gf, sh2, sc2, wr_t, tm)
        x = _moe(h2, aff, x1, gt2, wg, wu, wd)

        if not last:
            tqc = min(tq, l_ctx)
            oc_a = _attention(qa_c, None, None, kta_c, va_c, None, "none", tqc, tk)
            oc_b = _attention(qb_c, None, None, ktb_c, vb_c, sink_row(tqc), "none", tqc, tk)
            xc1, h2c, affc = _merge(oc_a, oc_b, bg_c, u_c, gts_c, xc, gtc1, conv_w[l], wb_bf, wo_bf, gf,
                                    shc2, scc2, wr_t, min(tm, l_ctx))
            xc = _moe(h2c, affc, xc1, gtc2, wg, wu, wd)
    return x
```

```python
import functools

import jax
import jax.numpy as jnp
from jax import lax
from jax.experimental import pallas as pl
from jax.experimental.pallas import tpu as pltpu

F32 = jnp.float32
BF16 = jnp.bfloat16
I32 = jnp.int32

HEAD_DIM = 64
N_HEADS = 8
N_KV = 2
GROUP = N_HEADS // N_KV
BRANCH = 512
N_EXPERTS = 16
CAPACITY_FACTOR = 2
GRID_W = 64
WINDOW = 128
ROPE_BASE = 10000.0
EPS = 1e-6
NEG_INF = -1e30

KV_W = N_KV * HEAD_DIM
OFF_QA = 4 * KV_W
OFF_QB = OFF_QA + N_HEADS * HEAD_DIM
OFF_CONV = OFF_QB + N_HEADS * HEAD_DIM
OFF_GATE = OFF_CONV + 3 * BRANCH

LANES = 128
V_ROWS = 80
MXU_DIM = 256
BF16_SUBLANES = 16
VMEM_LIMIT = 56 * 1024 * 1024


def _cparams(sem):
    return pltpu.CompilerParams(dimension_semantics=sem, vmem_limit_bytes=VMEM_LIMIT)


def _ada_kernel(s_ref, w_ref, b_ref, o_ref):
    s = s_ref[...]
    s = s * jax.nn.sigmoid(s)
    o_ref[0] = jnp.dot(s, w_ref[0], preferred_element_type=F32,
                       precision=lax.Precision.HIGHEST) + b_ref[0]


def _ada(s, w_ada, b_ada):
    depth, d, d6 = w_ada.shape
    rows = s.shape[0]
    tn = 1536
    return pl.pallas_call(
        _ada_kernel,
        grid=(depth, d6 // tn),
        in_specs=[
            pl.BlockSpec((rows, d), lambda l, j: (0, 0)),
            pl.BlockSpec((1, d, tn), lambda l, j: (l, 0, j)),
            pl.BlockSpec((1, 1, tn), lambda l, j: (l, 0, j)),
        ],
        out_specs=pl.BlockSpec((1, rows, tn), lambda l, j: (l, 0, j)),
        out_shape=jax.ShapeDtypeStruct((depth, rows, d6), F32),
        compiler_params=_cparams(("arbitrary", "arbitrary")),
        name="ada",
    )(s, w_ada, b_ada.reshape(depth, 1, d6))


def _inproj_kernel(x_ref, sh_ref, sc_ref, g_ref, w_ref, gq_ref, gk_ref, cs_ref, sn_ref,
                   qta_ref, qtb_ref, ka_ref, vta_ref, kb_ref, vtb_ref, bg_ref, u_ref, gt_ref):
    x = x_ref[0]
    tm = x.shape[0]
    ms = jnp.mean(x * x, axis=-1, keepdims=True)
    h = (x * lax.rsqrt(ms + EPS)) * g_ref[...]
    h = h * (1.0 + sc_ref[0]) + sh_ref[0]
    hb = h.astype(BF16)

    def proj(a, b):
        return jnp.dot(hb, w_ref[:, a:b], preferred_element_type=F32)

    slab = 2 * LANES
    r = lax.broadcasted_iota(I32, (slab, slab), 0) // HEAD_DIM
    c = lax.broadcasted_iota(I32, (slab, slab), 1) // HEAD_DIM
    head_mean = jnp.where(r == c, 1.0 / HEAD_DIM, 0.0).astype(BF16)
    cs = jnp.concatenate([cs_ref[...], cs_ref[...]], axis=1)
    sn = jnp.concatenate([sn_ref[...], sn_ref[...]], axis=1)
    lane = lax.broadcasted_iota(I32, (tm, slab), 1)
    first_half = (lane % HEAD_DIM) < (HEAD_DIM // 2)

    def head_norm_rope(p, gain):
        msq = jnp.dot((p * p).astype(BF16), head_mean, preferred_element_type=F32)
        y = p * lax.rsqrt(msq + EPS) * gain
        swapped = jnp.where(first_half, pltpu.roll(y, slab - HEAD_DIM // 2, 1), pltpu.roll(y, HEAD_DIM // 2, 1))
        return y * cs + swapped * sn

    pkv = proj(0, OFF_QA)
    k = jnp.concatenate([pkv[:, 0:KV_W], pkv[:, 2 * KV_W:3 * KV_W]], axis=1)
    k = head_norm_rope(k, gk_ref[...]).astype(BF16)
    ka_ref[0] = k[:, 0:KV_W]
    kb_ref[0] = k[:, KV_W:2 * KV_W]

    vt = jnp.transpose(jnp.concatenate([pkv[:, KV_W:2 * KV_W], pkv[:, 3 * KV_W:4 * KV_W]], axis=1))
    sub = lax.broadcasted_iota(I32, (V_ROWS - HEAD_DIM, tm), 0)
    ones_rows = jnp.where(sub == 0, 1.0, 0.0)
    for hh, ref in enumerate((vta_ref, vta_ref, vtb_ref, vtb_ref)):
        ref[0, hh % N_KV] = jnp.concatenate([vt[hh * HEAD_DIM:(hh + 1) * HEAD_DIM], ones_rows], axis=0).astype(BF16)

    pq = proj(OFF_QA, OFF_CONV)
    for s in range(4):
        q = head_norm_rope(pq[:, s * slab:(s + 1) * slab], gq_ref[:, s * slab:(s + 1) * slab])
        qt = jnp.transpose(q).astype(BF16)
        ref = qta_ref if s < 2 else qtb_ref
        for hh in range(GROUP):
            ref[0, (s % 2) * GROUP + hh] = qt[hh * HEAD_DIM:(hh + 1) * HEAD_DIM]

    pc = proj(OFF_CONV, OFF_GATE)
    bg_ref[0] = pc[:, 0:BRANCH].astype(BF16)
    u_ref[0] = (pc[:, BRANCH:2 * BRANCH] * pc[:, 2 * BRANCH:3 * BRANCH]).astype(BF16)

    d = x.shape[1]
    for j in range(3):
        pg = proj(OFF_GATE + j * d, OFF_GATE + (j + 1) * d)
        gt_ref[0, :, j * d:(j + 1) * d] = jax.nn.sigmoid(pg).astype(BF16)


def _inproj(x, sh, sc, g, w_bf, gq, gk, cs, sn, tm):
    b, n, d = x.shape
    pw = w_bf.shape[1]
    nq = N_HEADS * HEAD_DIM
    bmap = lambda bi, i: (bi, i, 0)
    qt_shape = jax.ShapeDtypeStruct((b, N_HEADS, HEAD_DIM, n), BF16)
    k_shape = jax.ShapeDtypeStruct((b, n, KV_W), BF16)
    vt_shape = jax.ShapeDtypeStruct((b, N_KV, V_ROWS, n), BF16)
    out_shape = [
        qt_shape, qt_shape, k_shape, vt_shape, k_shape, vt_shape,
        jax.ShapeDtypeStruct((b, n, BRANCH), BF16), jax.ShapeDtypeStruct((b, n, BRANCH), BF16),
        jax.ShapeDtypeStruct((b, n, 3 * d), BF16),
    ]
    qt_spec = pl.BlockSpec((1, N_HEADS, HEAD_DIM, tm), lambda bi, i: (bi, 0, 0, i))
    k_spec = pl.BlockSpec((1, tm, KV_W), bmap)
    vt_spec = pl.BlockSpec((1, N_KV, V_ROWS, tm), lambda bi, i: (bi, 0, 0, i))
    out_specs = [
        qt_spec, qt_spec, k_spec, vt_spec, k_spec, vt_spec,
        pl.BlockSpec((1, tm, BRANCH), bmap), pl.BlockSpec((1, tm, BRANCH), bmap),
        pl.BlockSpec((1, tm, 3 * d), bmap),
    ]
    in_specs = [
        pl.BlockSpec((1, tm, d), bmap),
        pl.BlockSpec((1, 1, d), lambda bi, i: (bi, 0, 0)),
        pl.BlockSpec((1, 1, d), lambda bi, i: (bi, 0, 0)),
        pl.BlockSpec((1, d), lambda bi, i: (0, 0)),
        pl.BlockSpec((d, pw), lambda bi, i: (0, 0), pipeline_mode=pl.Buffered(1)),
        pl.BlockSpec((1, 2 * nq), lambda bi, i: (0, 0)),
        pl.BlockSpec((1, 2 * LANES), lambda bi, i: (0, 0)),
        pl.BlockSpec((tm, LANES), lambda bi, i: (i, 0)),
        pl.BlockSpec((tm, LANES), lambda bi, i: (i, 0)),
    ]
    return pl.pallas_call(
        _inproj_kernel,
        grid=(b, n // tm),
        in_specs=in_specs,
        out_specs=out_specs,
        out_shape=out_shape,
        compiler_params=_cparams(("parallel", "parallel")),
        name="inproj",
    )(x, sh, sc, g, w_bf, gq, gk, cs, sn)


def _attn_kernel(*refs, mode, has_sink, tq, tk, n_lat):
    it = iter(refs)
    qt_ref = next(it)
    if mode == "global" and (n_lat // tk) % 2 == 0:
        qtn_ref = next(it)
    if mode != "none":
        kl_ref = next(it)
        vtl_ref = next(it)
    if mode != "global":
        kc_ref = next(it)
        vtc_ref = next(it)
    sink_ref = next(it) if has_sink else None
    o_ref = next(it)
    qp_scr = next(it)
    m_scr = next(it)
    acc_scr = next(it)
    if mode == "global":
        s_scr = next(it)
        cmax_scr = next(it)

    kv = pl.program_id(1)
    i = pl.program_id(2)
    chained = mode == "global" and (n_lat // tk) % 2 == 0
    slot = i % 2 if chained else 0

    def lanes_of(g):
        return slice(g * tq, (g + 1) * tq)

    def stage_queries(src_ref, dst):
        for g in range(GROUP):
            qt = src_ref[0, g]
            zero = jnp.zeros_like(qt)
            qp_scr[dst, 0:HEAD_DIM, lanes_of(g)] = jnp.where(kv == 0, qt, zero)
            qp_scr[dst, HEAD_DIM:2 * HEAD_DIM, lanes_of(g)] = jnp.where(kv == 0, zero, qt)

    if chained:
        @pl.when(i == 0)
        def _():
            stage_queries(qt_ref, slot)
        stage_queries(qtn_ref, 1 - slot)
    else:
        stage_queries(qt_ref, slot)
    if has_sink:
        m_scr[...] = sink_ref[0]
    else:
        m_scr[...] = jnp.full(m_scr.shape, NEG_INF, F32)
    acc_scr[...] = jnp.zeros(acc_scr.shape, F32)

    def scores(g, k, qslot=slot):
        return jnp.dot(k, qp_scr[qslot, :, lanes_of(g)], preferred_element_type=F32)

    def softmax(g, s, cmax):
        sl = lanes_of(g)
        m_prev = m_scr[:, sl]
        m_new = jnp.maximum(m_prev, cmax)
        m_scr[:, sl] = m_new
        return jnp.exp2(m_prev - m_new), jnp.exp2(s - m_new).astype(BF16)

    def weighted_values(g, alpha, p, vt):
        sl = lanes_of(g)
        acc_scr[:, sl] = alpha * acc_scr[:, sl] + jnp.dot(vt, p, preferred_element_type=F32)

    def softmax_pv(g, s, cmax, vt):
        alpha, p = softmax(g, s, cmax)
        weighted_values(g, alpha, p, vt)

    if mode == "global":
        nt = n_lat // tk

        def keys(j):
            return kl_ref[0, pl.ds(pl.multiple_of(j * tk, tk), tk), :]

        def values_t(j):
            return vtl_ref[0, 0, :, pl.ds(pl.multiple_of(j * tk, tk), tk)]

        def issue_scores(g, k, par, qslot=slot):
            s = scores(g, k, qslot)
            s_scr[par, g] = s
            cmax_scr[par, :, lanes_of(g)] = jnp.max(s, axis=0, keepdims=True)

        def first_scores():
            k0 = keys(0)
            for g in range(GROUP):
                issue_scores(g, k0, 0)

        if chained:
            pl.when(i == 0)(first_scores)
        else:
            first_scores()

        def consume(j, par):
            vt = values_t(j)
            for g in range(GROUP):
                softmax_pv(g, s_scr[par, g], cmax_scr[par, :, lanes_of(g)], vt)

        def pipelined(j, par):
            k_next = keys(j + 1)
            vt = values_t(j)
            for g in range(GROUP):
                alpha, p = softmax(g, s_scr[par, g], cmax_scr[par, :, lanes_of(g)])
                issue_scores(g, k_next, 1 - par)
                weighted_values(g, alpha, p, vt)

        def body(jj, carry):
            pipelined(2 * jj, 0)
            pipelined(2 * jj + 1, 1)
            return carry

        lax.fori_loop(0, (nt - 1) // 2, body, 0)
        if (nt - 1) % 2:
            pipelined(nt - 2, 0)
        if chained:
            @pl.when(i + 1 < pl.num_programs(2))
            def _():
                k0 = keys(0)
                vt = values_t(nt - 1)
                for g in range(GROUP):
                    alpha, p = softmax(g, s_scr[1, g], cmax_scr[1, :, lanes_of(g)])
                    issue_scores(g, k0, 0, 1 - slot)
                    weighted_values(g, alpha, p, vt)

            @pl.when(i + 1 == pl.num_programs(2))
            def _():
                consume(nt - 1, 1)
        else:
            consume(nt - 1, (nt - 1) % 2)
    else:
        if mode == "window":
            span = tq + 2 * WINDOW
            start = pl.multiple_of(jnp.clip(i * tq - WINDOW, 0, n_lat - span), LANES)
            l_ctx = kc_ref.shape[1]
            k = jnp.concatenate([kl_ref[0, pl.ds(start, span), :], kc_ref[0]], axis=0)
            vt = jnp.concatenate([vtl_ref[0, 0, :, pl.ds(start, span)], vtc_ref[0, 0]], axis=1)
            row = lax.broadcasted_iota(I32, (span + l_ctx, tq), 0)
            qpos = i * tq + lax.broadcasted_iota(I32, (span + l_ctx, tq), 1)
            mask = (row >= span) | (jnp.abs(qpos - (start + row)) <= WINDOW)
        else:
            k, vt, mask = kc_ref[0], vtc_ref[0, 0], None
        all_scores = [scores(g, k) for g in range(GROUP)]
        for g in range(GROUP):
            s = all_scores[g] if mask is None else jnp.where(mask, all_scores[g], NEG_INF)
            softmax_pv(g, s, jnp.max(s, axis=0, keepdims=True), vt)

    acc = acc_scr[...]
    denom = acc[HEAD_DIM:HEAD_DIM + 1, :]
    if has_sink:
        denom = denom + jnp.exp2(sink_ref[0] - m_scr[...])
    ot = acc / denom
    ot = jnp.concatenate([ot, jnp.zeros((LANES - V_ROWS, ot.shape[1]), F32)], axis=0)
    lane = lax.broadcasted_iota(I32, (tq, LANES), 1)
    heads = [jnp.transpose(ot[:, g * tq:(g + 1) * tq]) for g in range(GROUP)]
    pairs = [jnp.where(lane < HEAD_DIM, heads[2 * j], pltpu.roll(heads[2 * j + 1], HEAD_DIM, 1))
             for j in range(GROUP // 2)]
    o_ref[0] = jnp.concatenate(pairs, axis=1).astype(BF16)


def _attention(qt, k_lat, vt_lat, k_ctx, vt_ctx, sink, mode, tq, tk):
    b, _, _, nq = qt.shape
    n_lat = k_lat.shape[1] if mode != "none" else 0
    gw = GROUP * HEAD_DIM
    lanes = GROUP * tq
    kmap = lambda bi, kv, i: (bi, 0, 0)
    vmap = lambda bi, kv, i: (bi, kv, 0, 0)
    in_specs = [pl.BlockSpec((1, GROUP, HEAD_DIM, tq), lambda bi, kv, i: (bi, kv, 0, i))]
    args = [qt]
    chained = mode == "global" and (n_lat // tk) % 2 == 0
    if chained:
        last_tile = nq // tq - 1
        in_specs.append(pl.BlockSpec((1, GROUP, HEAD_DIM, tq),
                                     lambda bi, kv, i: (bi, kv, 0, jnp.minimum(i + 1, last_tile))))
        args.append(qt)
    if mode != "none":
        in_specs += [pl.BlockSpec((1, n_lat, KV_W), kmap), pl.BlockSpec((1, 1, V_ROWS, n_lat), vmap)]
        args += [k_lat, vt_lat]
    if mode != "global":
        l_ctx = k_ctx.shape[1]
        in_specs += [pl.BlockSpec((1, l_ctx, KV_W), kmap), pl.BlockSpec((1, 1, V_ROWS, l_ctx), vmap)]
        args += [k_ctx, vt_ctx]
    if sink is not None:
        in_specs.append(pl.BlockSpec((1, 1, lanes), lambda bi, kv, i: (kv, 0, 0)))
        args.append(sink)
    scratch = [pltpu.VMEM((2, 2 * HEAD_DIM, lanes), BF16), pltpu.VMEM((1, lanes), F32),
               pltpu.VMEM((V_ROWS, lanes), F32)]
    if mode == "global":
        assert n_lat % tk == 0
        scratch += [pltpu.VMEM((2, GROUP, tk, tq), F32), pltpu.VMEM((2, 1, lanes), F32)]
    kern = functools.partial(_attn_kernel, mode=mode, has_sink=sink is not None, tq=tq, tk=tk, n_lat=n_lat)
    return pl.pallas_call(
        kern,
        grid=(b, N_KV, nq // tq),
        in_specs=in_specs,
        out_specs=pl.BlockSpec((1, tq, gw), lambda bi, kv, i: (bi, i, kv)),
        out_shape=jax.ShapeDtypeStruct((b, nq, N_HEADS * HEAD_DIM), BF16),
        scratch_shapes=scratch,
        compiler_params=_cparams(("arbitrary",) * 3 if chained else ("parallel",) * 3),
        name="attn_" + mode + ("_sink" if sink is not None else ""),
    )(*args)


def _merge_kernel(oa_ref, ob_ref, bg_ref, u_ref, up_ref, un_ref, gts_ref, x_ref, gt1_ref, cw_ref, wb_ref, wo_ref,
                  gf_ref, sh2_ref, sc2_ref, wr_ref, x1_ref, h2_ref, aff_ref, *, nt):
    i = pl.program_id(1)
    tm, d = x_ref.shape[1], x_ref.shape[2]
    u = u_ref[0].astype(F32)
    prev_row = jnp.where(i > 0, up_ref[0][BF16_SUBLANES - 1:BF16_SUBLANES, :].astype(F32), 0.0)
    next_row = jnp.where(i < nt - 1, un_ref[0][0:1, :].astype(F32), 0.0)
    row = lax.broadcasted_iota(I32, (tm, BRANCH), 0)
    um1 = jnp.where(row == 0, prev_row, pltpu.roll(u, 1, 0))
    up1 = jnp.where(row == tm - 1, next_row, pltpu.roll(u, tm - 1, 0))
    cw = cw_ref[...]
    conv = cw[0:1] * um1 + cw[1:2] * u + cw[2:3] * up1
    oc = (bg_ref[0].astype(F32) * conv).astype(BF16)

    merged = gts_ref[0, :, 0:d].astype(F32) * jnp.dot(oa_ref[0], wb_ref[0], preferred_element_type=F32)
    merged += gts_ref[0, :, d:2 * d].astype(F32) * jnp.dot(ob_ref[0], wb_ref[1], preferred_element_type=F32)
    merged += gts_ref[0, :, 2 * d:3 * d].astype(F32) * jnp.dot(oc, wb_ref[2], preferred_element_type=F32)
    y = jnp.dot(merged.astype(BF16), wo_ref[...], preferred_element_type=F32)
    x1 = x_ref[0] + gt1_ref[0] * y
    x1_ref[0] = x1

    ms = jnp.mean(x1 * x1, axis=-1, keepdims=True)
    h2 = (x1 * lax.rsqrt(ms + EPS)) * gf_ref[...]
    h2 = h2 * (1.0 + sc2_ref[0]) + sh2_ref[0]
    h_hi = h2.astype(BF16)
    h2_ref[0] = h_hi

    h_lo = (h2 - h_hi.astype(F32)).astype(BF16)
    wr = wr_ref[...]
    wr_hi = wr.astype(BF16)
    wr_lo = (wr - wr_hi.astype(F32)).astype(BF16)
    dn = (((1,), (1,)), ((), ()))
    lg = lax.dot_general(wr_hi, h_hi, dn, preferred_element_type=F32)
    lg += lax.dot_general(wr_hi, h_lo, dn, preferred_element_type=F32)
    lg += lax.dot_general(wr_lo, h_hi, dn, preferred_element_type=F32)
    ex = jnp.exp(lg - jnp.max(lg, axis=0, keepdims=True))
    aff_ref[0] = ex / jnp.sum(ex, axis=0, keepdims=True)


def _merge(oa, ob, bg, u, gts, x, gt1, cw, wb_bf, wo_bf, gf, sh2, sc2, wr_t, tm):
    b, n, d = x.shape
    nt = n // tm
    e = wr_t.shape[0]
    bmap = lambda bi, i: (bi, i, 0)
    vec = lambda bi, i: (bi, 0, 0)
    halo = BF16_SUBLANES
    per = tm // halo
    last = n // halo - 1
    in_specs = [
        pl.BlockSpec((1, tm, BRANCH), bmap), pl.BlockSpec((1, tm, BRANCH), bmap),
        pl.BlockSpec((1, tm, BRANCH), bmap), pl.BlockSpec((1, tm, BRANCH), bmap),
        pl.BlockSpec((1, halo, BRANCH), lambda bi, i: (bi, jnp.maximum(i * per - 1, 0), 0)),
        pl.BlockSpec((1, halo, BRANCH), lambda bi, i: (bi, jnp.minimum((i + 1) * per, last), 0)),
        pl.BlockSpec((1, tm, 3 * d), bmap),
        pl.BlockSpec((1, tm, d), bmap),
        pl.BlockSpec((1, 1, d), vec),
        pl.BlockSpec((3, BRANCH), lambda bi, i: (0, 0)),
        pl.BlockSpec((3, BRANCH, d), lambda bi, i: (0, 0, 0)),
        pl.BlockSpec((d, d), lambda bi, i: (0, 0)),
        pl.BlockSpec((1, d), lambda bi, i: (0, 0)),
        pl.BlockSpec((1, 1, d), vec), pl.BlockSpec((1, 1, d), vec),
        pl.BlockSpec((e, d), lambda bi, i: (0, 0)),
    ]
    out_specs = [pl.BlockSpec((1, tm, d), bmap), pl.BlockSpec((1, tm, d), bmap),
                 pl.BlockSpec((1, e, tm), lambda bi, i: (bi, 0, i))]
    out_shape = [jax.ShapeDtypeStruct((b, n, d), F32), jax.ShapeDtypeStruct((b, n, d), BF16),
                 jax.ShapeDtypeStruct((b, e, n), F32)]
    return pl.pallas_call(
        functools.partial(_merge_kernel, nt=nt),
        grid=(b, nt),
        in_specs=in_specs,
        out_specs=out_specs,
        out_shape=out_shape,
        compiler_params=_cparams(("parallel", "parallel")),
        name="merge",
    )(oa, ob, bg, u, u, u, gts, x, gt1, cw, wb_bf, wo_bf, gf, sh2, sc2, wr_t)


META_P, META_LO, META_HI = 0, 64, 96


def _cumsum_lanes(x, chunk):
    r = lax.broadcasted_iota(I32, (chunk, chunk), 0)
    c = lax.broadcasted_iota(I32, (chunk, chunk), 1)
    upper = jnp.where(r <= c, 1.0, 0.0).astype(BF16)
    carry = jnp.zeros((x.shape[0], 1), F32)
    outs, starts = [], []
    for j in range(x.shape[1] // chunk):
        starts.append(carry)
        y = jnp.dot(x[:, j * chunk:(j + 1) * chunk].astype(BF16), upper, preferred_element_type=F32) + carry
        outs.append(y)
        carry = y[:, chunk - 1:chunk]
    return jnp.concatenate(outs, axis=1), starts


def _route_kernel(aff_ref, posm_ref, post_ref, meta_ref, *, cap, rt, chunk, gchunk):
    a = aff_ref[0]
    e, n = a.shape
    bits = pltpu.bitcast(a, I32)

    def search(it, cur):
        cand = cur | jnp.left_shift(jnp.int32(1), 30 - it)
        cnt = jnp.sum(jnp.where(bits >= cand, 1.0, 0.0), axis=1, keepdims=True)
        return jnp.where(cnt >= cap, cand, cur)

    tau = lax.fori_loop(0, 31, search, jnp.zeros((e, 1), I32))
    gt = bits > tau
    eq = bits == tau
    need = cap - jnp.sum(jnp.where(gt, 1.0, 0.0), axis=1, keepdims=True)
    eqf = jnp.where(eq, 1.0, 0.0)
    ceq, _ = _cumsum_lanes(eqf, chunk)
    sel = gt | (eq & ((ceq - eqf) < need))
    sf = jnp.where(sel, 1.0, 0.0)
    cin, starts = _cumsum_lanes(sf, chunk)
    posm = jnp.where(sel, cin - sf, -1.0)
    posm_ref[0] = posm
    pad = jnp.full((LANES - e, n), -1.0, F32)
    post_ref[0] = jnp.transpose(jnp.concatenate([posm, pad], axis=0))

    lane = lax.broadcasted_iota(I32, (e, LANES), 1)
    meta = jnp.zeros((e, LANES), F32)
    for t, st in enumerate(starts):
        meta = jnp.where(lane == META_P + t, st, meta)
    inv = 1.0 / gchunk
    for t in range(cap // rt):
        first = jnp.sum(jnp.where(cin <= float(rt * t), 1.0, 0.0), axis=1, keepdims=True)
        lastt = jnp.sum(jnp.where(cin < float(rt * (t + 1)), 1.0, 0.0), axis=1, keepdims=True)
        meta = jnp.where(lane == META_LO + t, jnp.floor(first * inv), meta)
        meta = jnp.where(lane == META_HI + t, jnp.floor(lastt * inv), meta)
    meta_ref[0] = meta


def _route(aff, cap, rt, chunk, gchunk):
    b, e, n = aff.shape
    assert n // chunk <= META_LO - META_P and cap // rt <= META_HI - META_LO
    return pl.pallas_call(
        functools.partial(_route_kernel, cap=cap, rt=rt, chunk=chunk, gchunk=gchunk),
        grid=(b,),
        in_specs=[pl.BlockSpec((1, e, n), lambda bi: (bi, 0, 0))],
        out_specs=[pl.BlockSpec((1, e, n), lambda bi: (bi, 0, 0)),
                   pl.BlockSpec((1, n, LANES), lambda bi: (bi, 0, 0)),
                   pl.BlockSpec((1, e, LANES), lambda bi: (bi, 0, 0))],
        out_shape=[jax.ShapeDtypeStruct((b, e, n), F32), jax.ShapeDtypeStruct((b, n, LANES), F32),
                   jax.ShapeDtypeStruct((b, e, LANES), F32)],
        compiler_params=_cparams(("parallel",)),
        name="route",
    )(aff)


def _ffn_kernel(lo_ref, hi_ref, posm_ref, aff_ref, h_ref, wg_ref, wu_ref, wd_ref, ys_ref, xs_scr, gate_scr,
                *, rt, chunk, ne, nt):
    bi, ei, ti = pl.program_id(0), pl.program_id(1), pl.program_id(2)
    lin = (bi * ne + ei) * nt + ti
    slot = (ti * rt + lax.broadcasted_iota(I32, (rt, 1), 0)).astype(F32)
    xs_scr[...] = jnp.zeros_like(xs_scr)
    gate_scr[...] = jnp.zeros_like(gate_scr)

    def body(c, carry):
        t0 = pl.multiple_of(c * chunk, chunk)
        hit = slot == posm_ref[0, :, pl.ds(t0, chunk)]
        xs_scr[...] += jnp.dot(jnp.where(hit, 1.0, 0.0).astype(BF16), h_ref[0, pl.ds(t0, chunk), :],
                               preferred_element_type=F32)
        gate_scr[...] += jnp.sum(jnp.where(hit, aff_ref[0, :, pl.ds(t0, chunk)], 0.0), axis=1, keepdims=True)
        return carry

    lax.fori_loop(lo_ref[lin], hi_ref[lin] + 1, body, 0)
    x = xs_scr[...].astype(BF16)
    a = jnp.dot(x, wg_ref[0], preferred_element_type=F32)
    u = jnp.dot(x, wu_ref[0], preferred_element_type=F32)
    act = (a * jax.nn.sigmoid(a) * u).astype(BF16)
    y = jnp.dot(act, wd_ref[0], preferred_element_type=F32)
    ys_ref[0, 0] = (y * gate_scr[...]).astype(BF16)


def _ffn(lo, hi, posm, aff, h2, wg, wu, wd, cap, rt, chunk):
    b, n, d = h2.shape
    e, _, hid = wg.shape
    nt = cap // rt
    rowmap = lambda bi, ei, ti, lo_r, hi_r: (bi * e + ei, 0, 0)
    wmap = lambda bi, ei, ti, lo_r, hi_r: (ei, 0, 0)
    grid_spec = pltpu.PrefetchScalarGridSpec(
        num_scalar_prefetch=2,
        grid=(b, e, nt),
        in_specs=[
            pl.BlockSpec((1, 1, n), rowmap),
            pl.BlockSpec((1, 1, n), rowmap),
            pl.BlockSpec((1, n, d), lambda bi, ei, ti, lo_r, hi_r: (bi, 0, 0), pipeline_mode=pl.Buffered(1)),
            pl.BlockSpec((1, d, hid), wmap),
            pl.BlockSpec((1, d, hid), wmap),
            pl.BlockSpec((1, hid, d), wmap),
        ],
        out_specs=pl.BlockSpec((1, 1, rt, d), lambda bi, ei, ti, lo_r, hi_r: (bi, ei, ti, 0)),
        scratch_shapes=[pltpu.VMEM((rt, d), F32), pltpu.VMEM((rt, 1), F32)],
    )
    return pl.pallas_call(
        functools.partial(_ffn_kernel, rt=rt, chunk=chunk, ne=e, nt=nt),
        grid_spec=grid_spec,
        out_shape=jax.ShapeDtypeStruct((b, e, cap, d), BF16),
        compiler_params=_cparams(("arbitrary", "arbitrary", "arbitrary")),
        name="ffn",
    )(lo, hi, posm.reshape(b * e, 1, n), aff.reshape(b * e, 1, n), h2, wg, wu, wd)


def _combine_kernel(*refs, ne, ntt, wm, wt):
    p0_ref, tail_ref, post_ref = refs[0:3]
    ys_refs = refs[3:3 + ne]
    x_ref, gt2_ref, o_ref, acc_scr = refs[3 + ne:]
    bi, ti = pl.program_id(0), pl.program_id(1)
    base = (bi * ntt + ti) * ne
    tt = x_ref.shape[1]
    pt = post_ref[0]
    col = lax.broadcasted_iota(I32, (tt, wm), 1).astype(F32)

    def rel(e):
        return pt[:, e:e + 1] - p0_ref[base + e].astype(F32)

    total = None
    for e in range(ne):
        hit = jnp.where(rel(e) == col, 1.0, 0.0).astype(BF16)
        part = jnp.dot(hit, ys_refs[e][0, 0, 0:wm, :], preferred_element_type=F32)
        total = part if total is None else total + part
    acc_scr[...] = total

    if wt:
        col2 = (wm + lax.broadcasted_iota(I32, (tt, wt), 1)).astype(F32)
        for e in range(ne):
            @pl.when(tail_ref[base + e] > 0)
            def _(e=e):
                hit2 = jnp.where(rel(e) == col2, 1.0, 0.0).astype(BF16)
                acc_scr[...] += jnp.dot(hit2, ys_refs[e][0, 0, wm:wm + wt, :], preferred_element_type=F32)

    o_ref[0] = x_ref[0] + gt2_ref[0] * acc_scr[...]


def _combine(p0, tail, post, ys, x1, gt2, tt, wm, wt):
    b, n, d = x1.shape
    ne = ys.shape[1]
    ntt = n // tt
    w = wm + wt

    def ys_spec(e):
        def ys_map(bi, ti, p0_r, tail_r):
            return (bi, e, pl.multiple_of(p0_r[(bi * ntt + ti) * ne + e], BF16_SUBLANES), 0)
        return pl.BlockSpec((pl.Element(1), pl.Element(1), pl.Element(w), pl.Element(d)), ys_map)

    tile = lambda bi, ti, p0_r, tail_r: (bi, ti, 0)
    grid_spec = pltpu.PrefetchScalarGridSpec(
        num_scalar_prefetch=2,
        grid=(b, ntt),
        in_specs=[pl.BlockSpec((1, tt, LANES), tile)] + [ys_spec(e) for e in range(ne)] + [
            pl.BlockSpec((1, tt, d), tile),
            pl.BlockSpec((1, 1, d), lambda bi, ti, p0_r, tail_r: (bi, 0, 0)),
        ],
        out_specs=pl.BlockSpec((1, tt, d), tile),
        scratch_shapes=[pltpu.VMEM((tt, d), F32)],
    )
    return pl.pallas_call(
        functools.partial(_combine_kernel, ne=ne, ntt=ntt, wm=wm, wt=wt),
        grid_spec=grid_spec,
        out_shape=jax.ShapeDtypeStruct((b, n, d), F32),
        compiler_params=_cparams(("arbitrary", "arbitrary")),
        name="combine",
    )(p0, tail, post, *([ys] * ne), x1, gt2)


def _moe(h2, aff, x1, gt2, wg, wu, wd):
    b, n, d = x1.shape
    e = aff.shape[1]
    cap = CAPACITY_FACTOR * n // e
    chunk = MXU_DIM
    rt = min(MXU_DIM, cap)
    gchunk = 2 * MXU_DIM if n % (2 * MXU_DIM) == 0 else MXU_DIM
    assert n % chunk == 0 and n % gchunk == 0 and cap % rt == 0
    posm, post, meta = _route(aff, cap, rt, chunk, gchunk)
    nt, ntt = cap // rt, n // chunk
    lo = meta[:, :, META_LO:META_LO + nt].astype(I32).reshape(-1)
    hi = meta[:, :, META_HI:META_HI + nt].astype(I32).reshape(-1)
    ys = _ffn(lo, hi, posm, aff, h2, wg, wu, wd, cap, rt, gchunk)

    wm = min(MXU_DIM, cap)
    wt = BF16_SUBLANES if cap >= MXU_DIM + BF16_SUBLANES else 0
    assert cap <= wm or wt > 0
    p = meta[:, :, META_P:META_P + ntt].astype(I32)
    p_next = jnp.concatenate([p[:, :, 1:], jnp.full((b, e, 1), cap, I32)], axis=2)
    p0 = jnp.clip((p // BF16_SUBLANES) * BF16_SUBLANES, 0, cap - (wm + wt))
    tail = (p_next > p0 + wm).astype(I32)
    p0 = jnp.transpose(p0, (0, 2, 1)).reshape(-1)
    tail = jnp.transpose(tail, (0, 2, 1)).reshape(-1)
    return _combine(p0, tail, post, ys, x1, gt2, chunk, wm, wt)


def _rope_tables(n):
    rows = n // GRID_W
    row = jnp.repeat(jnp.arange(rows, dtype=F32), GRID_W)
    col = jnp.tile(jnp.arange(GRID_W, dtype=F32), rows)
    n_freq = HEAD_DIM // 4
    inv = ROPE_BASE ** (-jnp.arange(n_freq, dtype=F32) / n_freq)
    ang = jnp.concatenate([row[:, None] * inv, col[:, None] * inv], axis=-1)
    cos, sin = jnp.cos(ang), jnp.sin(ang)
    cs = jnp.concatenate([cos, cos, cos, cos], axis=1)
    sn = jnp.concatenate([-sin, sin, -sin, sin], axis=1)
    return cs, sn


def _tile_sizes(n, l_ctx):
    tm = MXU_DIM
    tm_in = 2 * MXU_DIM if n % (2 * MXU_DIM) == 0 else MXU_DIM
    tq = MXU_DIM
    tk = next(t for t in (11 * LANES, 6 * LANES, 4 * LANES, 2 * LANES, LANES) if (n + l_ctx) % t == 0)
    assert n % tm_in == 0 and n >= tq + 2 * WINDOW and l_ctx % tq == 0
    return tm, tm_in, tq, tk


def kernel(x, c, ctx, c_ctx, w_ada, b_ada, g_mix, g_ffn, w_in, qg_a, kg_a, qg_b, kg_b, sink_b, conv_w, w_branch,
           w_out, w_router, w_e_gate, w_e_up, w_e_down):
    b, n, d = x.shape
    l_ctx = ctx.shape[1]
    depth = w_ada.shape[0]
    tm, tm_in, tq, tk = _tile_sizes(n, l_ctx)

    rows = -(-(b + 1) // 8) * 8
    s = jnp.concatenate([c, c_ctx[None, :], jnp.zeros((rows - b - 1, d), F32)], axis=0)
    mod_all = _ada(s, w_ada, b_ada)

    cs, sn = _rope_tables(n)
    cs_c = jnp.ones((l_ctx, LANES), F32)
    sn_c = jnp.zeros((l_ctx, LANES), F32)
    log2e = 1.4426950408889634
    scale = HEAD_DIM ** -0.5 * log2e

    xc = ctx
    for l in range(depth):
        last = l == depth - 1
        mod = mod_all[l]
        sh1, sc1, gt1, sh2, sc2, gt2 = [m[:, None, :] for m in jnp.split(mod[:b], 6, axis=-1)]
        shc1, scc1, gtc1, shc2, scc2, gtc2 = [jnp.broadcast_to(m[None, None, :], (b, 1, d))
                                              for m in jnp.split(mod[b], 6)]
        w_bf = w_in[l].astype(BF16)
        gq = (jnp.concatenate([jnp.tile(qg_a[l], N_HEADS), jnp.tile(qg_b[l], N_HEADS)]) * scale)[None, :]
        gk = jnp.concatenate([jnp.tile(kg_a[l], N_KV), jnp.tile(kg_b[l], N_KV)])[None, :]
        gm = g_mix[l][None, :]
        gf = g_ffn[l][None, :]
        wb_bf = w_branch[l].astype(BF16)
        wo_bf = w_out[l].astype(BF16)
        wr_t = jnp.transpose(w_router[l])
        wg, wu, wd = w_e_gate[l].astype(BF16), w_e_up[l].astype(BF16), w_e_down[l].astype(BF16)
        sink = sink_b[l].reshape(N_KV, GROUP) * log2e

        def sink_row(t):
            return jnp.repeat(sink, t, axis=1)[:, None, :]

        qa_c, qb_c, kta_c, va_c, ktb_c, vb_c, bg_c, u_c, gts_c = _inproj(
            xc, shc1, scc1, gm, w_bf, gq, gk, cs_c, sn_c, min(tm, l_ctx))
        qa, qb, kta, va, ktb, vb, bg, u, gts = _inproj(x, sh1, sc1, gm, w_bf, gq, gk, cs, sn, tm_in)

        k_all = jnp.concatenate([kta, kta_c], axis=1)
        vt_all = jnp.concatenate([va, va_c], axis=3)
        o_a = _attention(qa, k_all, vt_all, None, None, None, "global", tq, tk)
        o_b = _attention(qb, ktb, vb, ktb_c, vb_c, sink_row(tq), "window", tq, tk)
        x1, h2, aff = _merge(o_a, o_b, bg, u, gts, x, gt1, conv_w[l], wb_bf, wo_bf, gf, sh2, sc2, wr_t, tm)
        x = _moe(h2, aff, x1, gt2, wg, wu, wd)

        if not last:
            tqc = min(tq, l_ctx)
            oc_a = _attention(qa_c, None, None, kta_c, va_c, None, "none", tqc, tk)
            oc_b = _attention(qb_c, None, None, ktb_c, vb_c, sink_row(tqc), "none", tqc, tk)
            xc1, h2c, affc = _merge(oc_a, oc_b, bg_c, u_c, gts_c, xc, gtc1, conv_w[l], wb_bf, wo_bf, gf,
                                    shc2, scc2, wr_t, min(tm, l_ctx))
            xc = _moe(h2c, affc, xc1, gtc2, wg, wu, wd)
    return x
```

```python
import functools

import jax
import jax.numpy as jnp
from jax import lax
from jax.experimental import pallas as pl
from jax.experimental.pallas import tpu as pltpu

F32 = jnp.float32
BF16 = jnp.bfloat16
I32 = jnp.int32

HEAD_DIM = 64
N_HEADS = 8
N_KV = 2
GROUP = N_HEADS // N_KV
BRANCH = 512
N_EXPERTS = 16
CAPACITY_FACTOR = 2
GRID_W = 64
WINDOW = 128
ROPE_BASE = 10000.0
EPS = 1e-6
NEG_INF = -1e30

KV_W = N_KV * HEAD_DIM
OFF_QA = 4 * KV_W
OFF_QB = OFF_QA + N_HEADS * HEAD_DIM
OFF_CONV = OFF_QB + N_HEADS * HEAD_DIM
OFF_GATE = OFF_CONV + 3 * BRANCH

LANES = 128
V_ROWS = 80
MXU_DIM = 256
BF16_SUBLANES = 16
VMEM_LIMIT = 56 * 1024 * 1024


def _cparams(sem):
    return pltpu.CompilerParams(dimension_semantics=sem, vmem_limit_bytes=VMEM_LIMIT)


def _ada_kernel(s_ref, w_ref, b_ref, o_ref):
    s = s_ref[...]
    s = s * jax.nn.sigmoid(s)
    o_ref[0] = jnp.dot(s, w_ref[0], preferred_element_type=F32,
                       precision=lax.Precision.HIGHEST) + b_ref[0]


def _ada(s, w_ada, b_ada):
    depth, d, d6 = w_ada.shape
    rows = s.shape[0]
    tn = 1536
    return pl.pallas_call(
        _ada_kernel,
        grid=(depth, d6 // tn),
        in_specs=[
            pl.BlockSpec((rows, d), lambda l, j: (0, 0)),
            pl.BlockSpec((1, d, tn), lambda l, j: (l, 0, j)),
            pl.BlockSpec((1, 1, tn), lambda l, j: (l, 0, j)),
        ],
        out_specs=pl.BlockSpec((1, rows, tn), lambda l, j: (l, 0, j)),
        out_shape=jax.ShapeDtypeStruct((depth, rows, d6), F32),
        compiler_params=_cparams(("arbitrary", "arbitrary")),
        name="ada",
    )(s, w_ada, b_ada.reshape(depth, 1, d6))


def _inproj_kernel(x_ref, sh_ref, sc_ref, g_ref, w_ref, gq_ref, gk_ref, cs_ref, sn_ref,
                   qta_ref, qtb_ref, ka_ref, vta_ref, kb_ref, vtb_ref, bg_ref, u_ref, gt_ref):
    x = x_ref[0]
    tm = x.shape[0]
    ms = jnp.mean(x * x, axis=-1, keepdims=True)
    h = (x * lax.rsqrt(ms + EPS)) * g_ref[...]
    h = h * (1.0 + sc_ref[0]) + sh_ref[0]
    hb = h.astype(BF16)

    def proj(a, b):
        return jnp.dot(hb, w_ref[:, a:b], preferred_element_type=F32)

    slab = 2 * LANES
    r = lax.broadcasted_iota(I32, (slab, slab), 0) // HEAD_DIM
    c = lax.broadcasted_iota(I32, (slab, slab), 1) // HEAD_DIM
    head_mean = jnp.where(r == c, 1.0 / HEAD_DIM, 0.0).astype(BF16)
    cs = jnp.concatenate([cs_ref[...], cs_ref[...]], axis=1)
    sn = jnp.concatenate([sn_ref[...], sn_ref[...]], axis=1)
    lane = lax.broadcasted_iota(I32, (tm, slab), 1)
    first_half = (lane % HEAD_DIM) < (HEAD_DIM // 2)

    def head_norm_rope(p, gain):
        msq = jnp.dot((p * p).astype(BF16), head_mean, preferred_element_type=F32)
        y = p * lax.rsqrt(msq + EPS) * gain
        swapped = jnp.where(first_half, pltpu.roll(y, slab - HEAD_DIM // 2, 1), pltpu.roll(y, HEAD_DIM // 2, 1))
        return y * cs + swapped * sn

    pkv = proj(0, OFF_QA)
    k = jnp.concatenate([pkv[:, 0:KV_W], pkv[:, 2 * KV_W:3 * KV_W]], axis=1)
    k = head_norm_rope(k, gk_ref[...]).astype(BF16)
    ka_ref[0] = k[:, 0:KV_W]
    kb_ref[0] = k[:, KV_W:2 * KV_W]

    vt = jnp.transpose(jnp.concatenate([pkv[:, KV_W:2 * KV_W], pkv[:, 3 * KV_W:4 * KV_W]], axis=1))
    sub = lax.broadcasted_iota(I32, (V_ROWS - HEAD_DIM, tm), 0)
    ones_rows = jnp.where(sub == 0, 1.0, 0.0)
    for hh, ref in enumerate((vta_ref, vta_ref, vtb_ref, vtb_ref)):
        ref[0, hh % N_KV] = jnp.concatenate([vt[hh * HEAD_DIM:(hh + 1) * HEAD_DIM], ones_rows], axis=0).astype(BF16)

    pq = proj(OFF_QA, OFF_CONV)
    for s in range(4):
        q = head_norm_rope(pq[:, s * slab:(s + 1) * slab], gq_ref[:, s * slab:(s + 1) * slab])
        qt = jnp.transpose(q).astype(BF16)
        ref = qta_ref if s < 2 else qtb_ref
        for hh in range(GROUP):
            ref[0, (s % 2) * GROUP + hh] = qt[hh * HEAD_DIM:(hh + 1) * HEAD_DIM]

    pc = proj(OFF_CONV, OFF_GATE)
    bg_ref[0] = pc[:, 0:BRANCH].astype(BF16)
    u_ref[0] = (pc[:, BRANCH:2 * BRANCH] * pc[:, 2 * BRANCH:3 * BRANCH]).astype(BF16)

    d = x.shape[1]
    for j in range(3):
        pg = proj(OFF_GATE + j * d, OFF_GATE + (j + 1) * d)
        gt_ref[0, :, j * d:(j + 1) * d] = jax.nn.sigmoid(pg).astype(BF16)


def _inproj(x, sh, sc, g, w_bf, gq, gk, cs, sn, tm, layer):
    b, n, d = x.shape
    pw = w_bf.shape[2]
    nq = N_HEADS * HEAD_DIM
    bmap = lambda bi, i: (bi, i, 0)
    qt_shape = jax.ShapeDtypeStruct((b, N_HEADS, HEAD_DIM, n), BF16)
    k_shape = jax.ShapeDtypeStruct((b, n, KV_W), BF16)
    vt_shape = jax.ShapeDtypeStruct((b, N_KV, V_ROWS, n), BF16)
    out_shape = [
        qt_shape, qt_shape, k_shape, vt_shape, k_shape, vt_shape,
        jax.ShapeDtypeStruct((b, n, BRANCH), BF16), jax.ShapeDtypeStruct((b, n, BRANCH), BF16),
        jax.ShapeDtypeStruct((b, n, 3 * d), BF16),
    ]
    qt_spec = pl.BlockSpec((1, N_HEADS, HEAD_DIM, tm), lambda bi, i: (bi, 0, 0, i))
    k_spec = pl.BlockSpec((1, tm, KV_W), bmap)
    vt_spec = pl.BlockSpec((1, N_KV, V_ROWS, tm), lambda bi, i: (bi, 0, 0, i))
    out_specs = [
        qt_spec, qt_spec, k_spec, vt_spec, k_spec, vt_spec,
        pl.BlockSpec((1, tm, BRANCH), bmap), pl.BlockSpec((1, tm, BRANCH), bmap),
        pl.BlockSpec((1, tm, 3 * d), bmap),
    ]
    in_specs = [
        pl.BlockSpec((1, tm, d), bmap),
        pl.BlockSpec((1, 1, d), lambda bi, i: (bi, 0, 0)),
        pl.BlockSpec((1, 1, d), lambda bi, i: (bi, 0, 0)),
        pl.BlockSpec((1, d), lambda bi, i: (0, 0)),
        pl.BlockSpec((None, d, pw), lambda bi, i: (layer, 0, 0), pipeline_mode=pl.Buffered(1)),
        pl.BlockSpec((1, 2 * nq), lambda bi, i: (0, 0)),
        pl.BlockSpec((1, 2 * LANES), lambda bi, i: (0, 0)),
        pl.BlockSpec((tm, LANES), lambda bi, i: (i, 0)),
        pl.BlockSpec((tm, LANES), lambda bi, i: (i, 0)),
    ]
    return pl.pallas_call(
        _inproj_kernel,
        grid=(b, n // tm),
        in_specs=in_specs,
        out_specs=out_specs,
        out_shape=out_shape,
        compiler_params=_cparams(("parallel", "parallel")),
        name="inproj",
    )(x, sh, sc, g, w_bf, gq, gk, cs, sn)


def _attn_kernel(*refs, mode, has_sink, tq, tk, n_lat):
    it = iter(refs)
    qt_ref = next(it)
    if mode == "global" and (n_lat // tk) % 2 == 0:
        qtn_ref = next(it)
    if mode != "none":
        kl_ref = next(it)
        vtl_ref = next(it)
    if mode != "global":
        kc_ref = next(it)
        vtc_ref = next(it)
    sink_ref = next(it) if has_sink else None
    o_ref = next(it)
    qp_scr = next(it)
    m_scr = next(it)
    acc_scr = next(it)
    if mode == "global":
        s_scr = next(it)
        cmax_scr = next(it)

    kv = pl.program_id(1)
    i = pl.program_id(2)
    chained = mode == "global" and (n_lat // tk) % 2 == 0
    slot = i % 2 if chained else 0

    def lanes_of(g):
        return slice(g * tq, (g + 1) * tq)

    def stage_queries(src_ref, dst):
        for g in range(GROUP):
            qt = src_ref[0, g]
            zero = jnp.zeros_like(qt)
            qp_scr[dst, 0:HEAD_DIM, lanes_of(g)] = jnp.where(kv == 0, qt, zero)
            qp_scr[dst, HEAD_DIM:2 * HEAD_DIM, lanes_of(g)] = jnp.where(kv == 0, zero, qt)

    if chained:
        @pl.when(i == 0)
        def _():
            stage_queries(qt_ref, slot)
        stage_queries(qtn_ref, 1 - slot)
    else:
        stage_queries(qt_ref, slot)
    if has_sink:
        m_scr[...] = sink_ref[0]
    else:
        m_scr[...] = jnp.full(m_scr.shape, NEG_INF, F32)
    acc_scr[...] = jnp.zeros(acc_scr.shape, F32)

    def scores(g, k, qslot=slot):
        return jnp.dot(k, qp_scr[qslot, :, lanes_of(g)], preferred_element_type=F32)

    def softmax(g, s, cmax):
        sl = lanes_of(g)
        m_prev = m_scr[:, sl]
        m_new = jnp.maximum(m_prev, cmax)
        m_scr[:, sl] = m_new
        return jnp.exp2(m_prev - m_new), jnp.exp2(s - m_new).astype(BF16)

    def weighted_values(g, alpha, p, vt):
        sl = lanes_of(g)
        acc_scr[:, sl] = alpha * acc_scr[:, sl] + jnp.dot(vt, p, preferred_element_type=F32)

    def softmax_pv(g, s, cmax, vt):
        alpha, p = softmax(g, s, cmax)
        weighted_values(g, alpha, p, vt)

    if mode == "global":
        nt = n_lat // tk

        def keys(j):
            return kl_ref[0, pl.ds(pl.multiple_of(j * tk, tk), tk), :]

        def values_t(j):
            return vtl_ref[0, 0, :, pl.ds(pl.multiple_of(j * tk, tk), tk)]

        def issue_scores(g, k, par, qslot=slot):
            s = scores(g, k, qslot)
            s_scr[par, g] = s
            cmax_scr[par, :, lanes_of(g)] = jnp.max(s, axis=0, keepdims=True)

        def first_scores():
            k0 = keys(0)
            for g in range(GROUP):
                issue_scores(g, k0, 0)

        if chained:
            pl.when(i == 0)(first_scores)
        else:
            first_scores()

        def consume(j, par):
            vt = values_t(j)
            for g in range(GROUP):
                softmax_pv(g, s_scr[par, g], cmax_scr[par, :, lanes_of(g)], vt)

        def pipelined(j, par):
            k_next = keys(j + 1)
            vt = values_t(j)
            for g in range(GROUP):
                alpha, p = softmax(g, s_scr[par, g], cmax_scr[par, :, lanes_of(g)])
                issue_scores(g, k_next, 1 - par)
                weighted_values(g, alpha, p, vt)

        def body(jj, carry):
            pipelined(2 * jj, 0)
            pipelined(2 * jj + 1, 1)
            return carry

        lax.fori_loop(0, (nt - 1) // 2, body, 0)
        if (nt - 1) % 2:
            pipelined(nt - 2, 0)
        if chained:
            @pl.when(i + 1 < pl.num_programs(2))
            def _():
                k0 = keys(0)
                vt = values_t(nt - 1)
                for g in range(GROUP):
                    alpha, p = softmax(g, s_scr[1, g], cmax_scr[1, :, lanes_of(g)])
                    issue_scores(g, k0, 0, 1 - slot)
                    weighted_values(g, alpha, p, vt)

            @pl.when(i + 1 == pl.num_programs(2))
            def _():
                consume(nt - 1, 1)
        else:
            consume(nt - 1, (nt - 1) % 2)
    else:
        if mode == "window":
            span = tq + 2 * WINDOW
            start = pl.multiple_of(jnp.clip(i * tq - WINDOW, 0, n_lat - span), LANES)
            l_ctx = kc_ref.shape[1]
            k = jnp.concatenate([kl_ref[0, pl.ds(start, span), :], kc_ref[0]], axis=0)
            vt = jnp.concatenate([vtl_ref[0, 0, :, pl.ds(start, span)], vtc_ref[0, 0]], axis=1)
            row = lax.broadcasted_iota(I32, (span + l_ctx, tq), 0)
            qpos = i * tq + lax.broadcasted_iota(I32, (span + l_ctx, tq), 1)
            mask = (row >= span) | (jnp.abs(qpos - (start + row)) <= WINDOW)
        else:
            k, vt, mask = kc_ref[0], vtc_ref[0, 0], None
        all_scores = [scores(g, k) for g in range(GROUP)]
        for g in range(GROUP):
            s = all_scores[g] if mask is None else jnp.where(mask, all_scores[g], NEG_INF)
            softmax_pv(g, s, jnp.max(s, axis=0, keepdims=True), vt)

    acc = acc_scr[...]
    denom = acc[HEAD_DIM:HEAD_DIM + 1, :]
    if has_sink:
        denom = denom + jnp.exp2(sink_ref[0] - m_scr[...])
    ot = acc / denom
    ot = jnp.concatenate([ot, jnp.zeros((LANES - V_ROWS, ot.shape[1]), F32)], axis=0)
    lane = lax.broadcasted_iota(I32, (tq, LANES), 1)
    heads = [jnp.transpose(ot[:, g * tq:(g + 1) * tq]) for g in range(GROUP)]
    pairs = [jnp.where(lane < HEAD_DIM, heads[2 * j], pltpu.roll(heads[2 * j + 1], HEAD_DIM, 1))
             for j in range(GROUP // 2)]
    o_ref[0] = jnp.concatenate(pairs, axis=1).astype(BF16)


def _attention(qt, k_lat, vt_lat, k_ctx, vt_ctx, sink, mode, tq, tk):
    b, _, _, nq = qt.shape
    n_lat = k_lat.shape[1] if mode != "none" else 0
    gw = GROUP * HEAD_DIM
    lanes = GROUP * tq
    kmap = lambda bi, kv, i: (bi, 0, 0)
    vmap = lambda bi, kv, i: (bi, kv, 0, 0)
    in_specs = [pl.BlockSpec((1, GROUP, HEAD_DIM, tq), lambda bi, kv, i: (bi, kv, 0, i))]
    args = [qt]
    chained = mode == "global" and (n_lat // tk) % 2 == 0
    if chained:
        last_tile = nq // tq - 1
        in_specs.append(pl.BlockSpec((1, GROUP, HEAD_DIM, tq),
                                     lambda bi, kv, i: (bi, kv, 0, jnp.minimum(i + 1, last_tile))))
        args.append(qt)
    if mode != "none":
        in_specs += [pl.BlockSpec((1, n_lat, KV_W), kmap), pl.BlockSpec((1, 1, V_ROWS, n_lat), vmap)]
        args += [k_lat, vt_lat]
    if mode != "global":
        l_ctx = k_ctx.shape[1]
        in_specs += [pl.BlockSpec((1, l_ctx, KV_W), kmap), pl.BlockSpec((1, 1, V_ROWS, l_ctx), vmap)]
        args += [k_ctx, vt_ctx]
    if sink is not None:
        in_specs.append(pl.BlockSpec((1, 1, lanes), lambda bi, kv, i: (kv, 0, 0)))
        args.append(sink)
    scratch = [pltpu.VMEM((2, 2 * HEAD_DIM, lanes), BF16), pltpu.VMEM((1, lanes), F32),
               pltpu.VMEM((V_ROWS, lanes), F32)]
    if mode == "global":
        assert n_lat % tk == 0
        scratch += [pltpu.VMEM((2, GROUP, tk, tq), F32), pltpu.VMEM((2, 1, lanes), F32)]
    kern = functools.partial(_attn_kernel, mode=mode, has_sink=sink is not None, tq=tq, tk=tk, n_lat=n_lat)
    return pl.pallas_call(
        kern,
        grid=(b, N_KV, nq // tq),
        in_specs=in_specs,
        out_specs=pl.BlockSpec((1, tq, gw), lambda bi, kv, i: (bi, i, kv)),
        out_shape=jax.ShapeDtypeStruct((b, nq, N_HEADS * HEAD_DIM), BF16),
        scratch_shapes=scratch,
        compiler_params=_cparams(("arbitrary",) * 3 if chained else ("parallel",) * 3),
        name="attn_" + mode + ("_sink" if sink is not None else ""),
    )(*args)


def _merge_kernel(oa_ref, ob_ref, bg_ref, u_ref, up_ref, un_ref, gts_ref, x_ref, gt1_ref, cw_ref, wb_ref, wo_ref,
                  gf_ref, sh2_ref, sc2_ref, wr_ref, x1_ref, h2_ref, aff_ref, *, nt):
    i = pl.program_id(1)
    tm, d = x_ref.shape[1], x_ref.shape[2]
    u = u_ref[0].astype(F32)
    prev_row = jnp.where(i > 0, up_ref[0][BF16_SUBLANES - 1:BF16_SUBLANES, :].astype(F32), 0.0)
    next_row = jnp.where(i < nt - 1, un_ref[0][0:1, :].astype(F32), 0.0)
    row = lax.broadcasted_iota(I32, (tm, BRANCH), 0)
    um1 = jnp.where(row == 0, prev_row, pltpu.roll(u, 1, 0))
    up1 = jnp.where(row == tm - 1, next_row, pltpu.roll(u, tm - 1, 0))
    cw = cw_ref[...]
    conv = cw[0:1] * um1 + cw[1:2] * u + cw[2:3] * up1
    oc = (bg_ref[0].astype(F32) * conv).astype(BF16)

    merged = gts_ref[0, :, 0:d].astype(F32) * jnp.dot(oa_ref[0], wb_ref[0], preferred_element_type=F32)
    merged += gts_ref[0, :, d:2 * d].astype(F32) * jnp.dot(ob_ref[0], wb_ref[1], preferred_element_type=F32)
    merged += gts_ref[0, :, 2 * d:3 * d].astype(F32) * jnp.dot(oc, wb_ref[2], preferred_element_type=F32)
    y = jnp.dot(merged.astype(BF16), wo_ref[...], preferred_element_type=F32)
    x1 = x_ref[0] + gt1_ref[0] * y
    x1_ref[0] = x1

    ms = jnp.mean(x1 * x1, axis=-1, keepdims=True)
    h2 = (x1 * lax.rsqrt(ms + EPS)) * gf_ref[...]
    h2 = h2 * (1.0 + sc2_ref[0]) + sh2_ref[0]
    h_hi = h2.astype(BF16)
    h2_ref[0] = h_hi

    h_lo = (h2 - h_hi.astype(F32)).astype(BF16)
    wr = wr_ref[...]
    wr_hi = wr.astype(BF16)
    wr_lo = (wr - wr_hi.astype(F32)).astype(BF16)
    dn = (((1,), (1,)), ((), ()))
    lg = lax.dot_general(wr_hi, h_hi, dn, preferred_element_type=F32)
    lg += lax.dot_general(wr_hi, h_lo, dn, preferred_element_type=F32)
    lg += lax.dot_general(wr_lo, h_hi, dn, preferred_element_type=F32)
    ex = jnp.exp(lg - jnp.max(lg, axis=0, keepdims=True))
    aff_ref[0] = ex / jnp.sum(ex, axis=0, keepdims=True)


def _merge(oa, ob, bg, u, gts, x, gt1, cw, wb_bf, wo_bf, gf, sh2, sc2, wr_t, tm):
    b, n, d = x.shape
    nt = n // tm
    e = wr_t.shape[0]
    bmap = lambda bi, i: (bi, i, 0)
    vec = lambda bi, i: (bi, 0, 0)
    halo = BF16_SUBLANES
    per = tm // halo
    last = n // halo - 1
    in_specs = [
        pl.BlockSpec((1, tm, BRANCH), bmap), pl.BlockSpec((1, tm, BRANCH), bmap),
        pl.BlockSpec((1, tm, BRANCH), bmap), pl.BlockSpec((1, tm, BRANCH), bmap),
        pl.BlockSpec((1, halo, BRANCH), lambda bi, i: (bi, jnp.maximum(i * per - 1, 0), 0)),
        pl.BlockSpec((1, halo, BRANCH), lambda bi, i: (bi, jnp.minimum((i + 1) * per, last), 0)),
        pl.BlockSpec((1, tm, 3 * d), bmap),
        pl.BlockSpec((1, tm, d), bmap),
        pl.BlockSpec((1, 1, d), vec),
        pl.BlockSpec((3, BRANCH), lambda bi, i: (0, 0)),
        pl.BlockSpec((3, BRANCH, d), lambda bi, i: (0, 0, 0)),
        pl.BlockSpec((d, d), lambda bi, i: (0, 0)),
        pl.BlockSpec((1, d), lambda bi, i: (0, 0)),
        pl.BlockSpec((1, 1, d), vec), pl.BlockSpec((1, 1, d), vec),
        pl.BlockSpec((e, d), lambda bi, i: (0, 0)),
    ]
    out_specs = [pl.BlockSpec((1, tm, d), bmap), pl.BlockSpec((1, tm, d), bmap),
                 pl.BlockSpec((1, e, tm), lambda bi, i: (bi, 0, i))]
    out_shape = [jax.ShapeDtypeStruct((b, n, d), F32), jax.ShapeDtypeStruct((b, n, d), BF16),
                 jax.ShapeDtypeStruct((b, e, n), F32)]
    return pl.pallas_call(
        functools.partial(_merge_kernel, nt=nt),
        grid=(b, nt),
        in_specs=in_specs,
        out_specs=out_specs,
        out_shape=out_shape,
        compiler_params=_cparams(("parallel", "parallel")),
        name="merge",
    )(oa, ob, bg, u, u, u, gts, x, gt1, cw, wb_bf, wo_bf, gf, sh2, sc2, wr_t)


META_P, META_LO, META_HI = 0, 64, 96


def _cumsum_lanes(x, chunk):
    r = lax.broadcasted_iota(I32, (chunk, chunk), 0)
    c = lax.broadcasted_iota(I32, (chunk, chunk), 1)
    upper = jnp.where(r <= c, 1.0, 0.0).astype(BF16)
    carry = jnp.zeros((x.shape[0], 1), F32)
    outs, starts = [], []
    for j in range(x.shape[1] // chunk):
        starts.append(carry)
        y = jnp.dot(x[:, j * chunk:(j + 1) * chunk].astype(BF16), upper, preferred_element_type=F32) + carry
        outs.append(y)
        carry = y[:, chunk - 1:chunk]
    return jnp.concatenate(outs, axis=1), starts


def _route_kernel(aff_ref, posm_ref, affrow_ref, post_ref, meta_ref, *, cap, rt, chunk, gchunk):
    a = aff_ref[0]
    e, n = a.shape
    bits = pltpu.bitcast(a, I32)

    def search(it, cur):
        cand = cur | jnp.left_shift(jnp.int32(1), 30 - it)
        cnt = jnp.sum(jnp.where(bits >= cand, 1.0, 0.0), axis=1, keepdims=True)
        return jnp.where(cnt >= cap, cand, cur)

    tau = lax.fori_loop(0, 31, search, jnp.zeros((e, 1), I32))
    gt = bits > tau
    eq = bits == tau
    need = cap - jnp.sum(jnp.where(gt, 1.0, 0.0), axis=1, keepdims=True)
    eqf = jnp.where(eq, 1.0, 0.0)
    ceq, _ = _cumsum_lanes(eqf, chunk)
    sel = gt | (eq & ((ceq - eqf) < need))
    sf = jnp.where(sel, 1.0, 0.0)
    cin, starts = _cumsum_lanes(sf, chunk)
    posm = jnp.where(sel, cin - sf, -1.0)
    posm_ref[:, 0, :] = posm
    affrow_ref[:, 0, :] = a
    pad = jnp.full((LANES - e, n), -1.0, F32)
    post_ref[0] = jnp.transpose(jnp.concatenate([posm, pad], axis=0))

    lane = lax.broadcasted_iota(I32, (e, LANES), 1)
    meta = jnp.zeros((e, LANES), F32)
    for t, st in enumerate(starts):
        meta = jnp.where(lane == META_P + t, st, meta)
    inv = 1.0 / gchunk
    for t in range(cap // rt):
        first = jnp.sum(jnp.where(cin <= float(rt * t), 1.0, 0.0), axis=1, keepdims=True)
        lastt = jnp.sum(jnp.where(cin < float(rt * (t + 1)), 1.0, 0.0), axis=1, keepdims=True)
        meta = jnp.where(lane == META_LO + t, jnp.floor(first * inv), meta)
        meta = jnp.where(lane == META_HI + t, jnp.floor(lastt * inv), meta)
    meta_ref[0] = meta


def _route(aff, cap, rt, chunk, gchunk):
    b, e, n = aff.shape
    assert n // chunk <= META_LO - META_P and cap // rt <= META_HI - META_LO
    return pl.pallas_call(
        functools.partial(_route_kernel, cap=cap, rt=rt, chunk=chunk, gchunk=gchunk),
        grid=(b,),
        in_specs=[pl.BlockSpec((1, e, n), lambda bi: (bi, 0, 0))],
        out_specs=[pl.BlockSpec((e, 1, n), lambda bi: (bi, 0, 0)),
                   pl.BlockSpec((e, 1, n), lambda bi: (bi, 0, 0)),
                   pl.BlockSpec((1, n, LANES), lambda bi: (bi, 0, 0)),
                   pl.BlockSpec((1, e, LANES), lambda bi: (bi, 0, 0))],
        out_shape=[jax.ShapeDtypeStruct((b * e, 1, n), F32), jax.ShapeDtypeStruct((b * e, 1, n), F32),
                   jax.ShapeDtypeStruct((b, n, LANES), F32), jax.ShapeDtypeStruct((b, e, LANES), F32)],
        compiler_params=_cparams(("parallel",)),
        name="route",
    )(aff)


def _ffn_kernel(lo_ref, hi_ref, posm_ref, aff_ref, h_ref, wg_ref, wu_ref, wd_ref, ys_ref, xs_scr, gate_scr,
                *, rt, chunk, ne, nt):
    bi, ei, ti = pl.program_id(0), pl.program_id(1), pl.program_id(2)
    lin = (bi * ne + ei) * nt + ti
    slot = (ti * rt + lax.broadcasted_iota(I32, (rt, 1), 0)).astype(F32)
    xs_scr[...] = jnp.zeros_like(xs_scr)
    gate_scr[...] = jnp.zeros_like(gate_scr)

    def body(c, carry):
        t0 = pl.multiple_of(c * chunk, chunk)
        hit = slot == posm_ref[0, :, pl.ds(t0, chunk)]
        xs_scr[...] += jnp.dot(jnp.where(hit, 1.0, 0.0).astype(BF16), h_ref[0, pl.ds(t0, chunk), :],
                               preferred_element_type=F32)
        gate_scr[...] += jnp.sum(jnp.where(hit, aff_ref[0, :, pl.ds(t0, chunk)], 0.0), axis=1, keepdims=True)
        return carry

    lax.fori_loop(lo_ref[lin], hi_ref[lin] + 1, body, 0)
    x = xs_scr[...].astype(BF16)
    a = jnp.dot(x, wg_ref[0], preferred_element_type=F32)
    u = jnp.dot(x, wu_ref[0], preferred_element_type=F32)
    act = (a * jax.nn.sigmoid(a) * u).astype(BF16)
    y = jnp.dot(act, wd_ref[0], preferred_element_type=F32)
    ys_ref[0, 0] = (y * gate_scr[...]).astype(BF16)


def _ffn(lo, hi, posm, aff, h2, wg, wu, wd, cap, rt, chunk, layer):
    b, n, d = h2.shape
    _, e, _, hid = wg.shape
    nt = cap // rt
    rowmap = lambda bi, ei, ti, lo_r, hi_r: (bi * e + ei, 0, 0)
    wmap = lambda bi, ei, ti, lo_r, hi_r: (layer, ei, 0, 0)
    grid_spec = pltpu.PrefetchScalarGridSpec(
        num_scalar_prefetch=2,
        grid=(b, e, nt),
        in_specs=[
            pl.BlockSpec((1, 1, n), rowmap),
            pl.BlockSpec((1, 1, n), rowmap),
            pl.BlockSpec((1, n, d), lambda bi, ei, ti, lo_r, hi_r: (bi, 0, 0), pipeline_mode=pl.Buffered(1)),
            pl.BlockSpec((None, 1, d, hid), wmap),
            pl.BlockSpec((None, 1, d, hid), wmap),
            pl.BlockSpec((None, 1, hid, d), wmap),
        ],
        out_specs=pl.BlockSpec((1, 1, rt, d), lambda bi, ei, ti, lo_r, hi_r: (bi, ei, ti, 0)),
        scratch_shapes=[pltpu.VMEM((rt, d), F32), pltpu.VMEM((rt, 1), F32)],
    )
    return pl.pallas_call(
        functools.partial(_ffn_kernel, rt=rt, chunk=chunk, ne=e, nt=nt),
        grid_spec=grid_spec,
        out_shape=jax.ShapeDtypeStruct((b, e, cap, d), BF16),
        compiler_params=_cparams(("arbitrary", "arbitrary", "arbitrary")),
        name="ffn",
    )(lo, hi, posm, aff, h2, wg, wu, wd)


def _combine_kernel(*refs, ne, ntt, wm, wt):
    p0_ref, tail_ref, post_ref = refs[0:3]
    ys_refs = refs[3:3 + ne]
    x_ref, gt2_ref, o_ref, acc_scr = refs[3 + ne:]
    bi, ti = pl.program_id(0), pl.program_id(1)
    base = (bi * ntt + ti) * ne
    tt = x_ref.shape[1]
    pt = post_ref[0]
    col = lax.broadcasted_iota(I32, (tt, wm), 1).astype(F32)

    def rel(e):
        return pt[:, e:e + 1] - p0_ref[base + e].astype(F32)

    total = None
    for e in range(ne):
        hit = jnp.where(rel(e) == col, 1.0, 0.0).astype(BF16)
        part = jnp.dot(hit, ys_refs[e][0, 0, 0:wm, :], preferred_element_type=F32)
        total = part if total is None else total + part
    acc_scr[...] = total

    if wt:
        col2 = (wm + lax.broadcasted_iota(I32, (tt, wt), 1)).astype(F32)
        for e in range(ne):
            @pl.when(tail_ref[base + e] > 0)
            def _(e=e):
                hit2 = jnp.where(rel(e) == col2, 1.0, 0.0).astype(BF16)
                acc_scr[...] += jnp.dot(hit2, ys_refs[e][0, 0, wm:wm + wt, :], preferred_element_type=F32)

    o_ref[0] = x_ref[0] + gt2_ref[0] * acc_scr[...]


def _combine(p0, tail, post, ys, x1, gt2, tt, wm, wt):
    b, n, d = x1.shape
    ne = ys.shape[1]
    ntt = n // tt
    w = wm + wt

    def ys_spec(e):
        def ys_map(bi, ti, p0_r, tail_r):
            return (bi, e, pl.multiple_of(p0_r[(bi * ntt + ti) * ne + e], BF16_SUBLANES), 0)
        return pl.BlockSpec((pl.Element(1), pl.Element(1), pl.Element(w), pl.Element(d)), ys_map)

    tile = lambda bi, ti, p0_r, tail_r: (bi, ti, 0)
    grid_spec = pltpu.PrefetchScalarGridSpec(
        num_scalar_prefetch=2,
        grid=(b, ntt),
        in_specs=[pl.BlockSpec((1, tt, LANES), tile)] + [ys_spec(e) for e in range(ne)] + [
            pl.BlockSpec((1, tt, d), tile),
            pl.BlockSpec((1, 1, d), lambda bi, ti, p0_r, tail_r: (bi, 0, 0)),
        ],
        out_specs=pl.BlockSpec((1, tt, d), tile),
        scratch_shapes=[pltpu.VMEM((tt, d), F32)],
    )
    return pl.pallas_call(
        functools.partial(_combine_kernel, ne=ne, ntt=ntt, wm=wm, wt=wt),
        grid_spec=grid_spec,
        out_shape=jax.ShapeDtypeStruct((b, n, d), F32),
        compiler_params=_cparams(("arbitrary", "arbitrary")),
        name="combine",
    )(p0, tail, post, *([ys] * ne), x1, gt2)


def _moe(h2, aff, x1, gt2, wg, wu, wd, layer):
    b, n, d = x1.shape
    e = aff.shape[1]
    cap = CAPACITY_FACTOR * n // e
    chunk = MXU_DIM
    rt = min(MXU_DIM, cap)
    gchunk = 2 * MXU_DIM if n % (2 * MXU_DIM) == 0 else MXU_DIM
    assert n % chunk == 0 and n % gchunk == 0 and cap % rt == 0
    posm, aff_rows, post, meta = _route(aff, cap, rt, chunk, gchunk)
    nt, ntt = cap // rt, n // chunk
    lo = meta[:, :, META_LO:META_LO + nt].astype(I32).reshape(-1)
    hi = meta[:, :, META_HI:META_HI + nt].astype(I32).reshape(-1)
    ys = _ffn(lo, hi, posm, aff_rows, h2, wg, wu, wd, cap, rt, gchunk, layer)

    wm = min(MXU_DIM, cap)
    wt = BF16_SUBLANES if cap >= MXU_DIM + BF16_SUBLANES else 0
    assert cap <= wm or wt > 0
    p = meta[:, :, META_P:META_P + ntt].astype(I32)
    p_next = jnp.concatenate([p[:, :, 1:], jnp.full((b, e, 1), cap, I32)], axis=2)
    p0 = jnp.clip((p // BF16_SUBLANES) * BF16_SUBLANES, 0, cap - (wm + wt))
    tail = (p_next > p0 + wm).astype(I32)
    p0 = jnp.transpose(p0, (0, 2, 1)).reshape(-1)
    tail = jnp.transpose(tail, (0, 2, 1)).reshape(-1)
    return _combine(p0, tail, post, ys, x1, gt2, chunk, wm, wt)


def _rope_tables(n):
    rows = n // GRID_W
    row = jnp.repeat(jnp.arange(rows, dtype=F32), GRID_W)
    col = jnp.tile(jnp.arange(GRID_W, dtype=F32), rows)
    n_freq = HEAD_DIM // 4
    inv = ROPE_BASE ** (-jnp.arange(n_freq, dtype=F32) / n_freq)
    ang = jnp.concatenate([row[:, None] * inv, col[:, None] * inv], axis=-1)
    cos, sin = jnp.cos(ang), jnp.sin(ang)
    cs = jnp.concatenate([cos, cos, cos, cos], axis=1)
    sn = jnp.concatenate([-sin, sin, -sin, sin], axis=1)
    return cs, sn


def _tile_sizes(n, l_ctx):
    tm = MXU_DIM
    tm_in = 2 * MXU_DIM if n % (2 * MXU_DIM) == 0 else MXU_DIM
    tq = MXU_DIM
    tk = next(t for t in (11 * LANES, 6 * LANES, 4 * LANES, 2 * LANES, LANES) if (n + l_ctx) % t == 0)
    assert n % tm_in == 0 and n >= tq + 2 * WINDOW and l_ctx % tq == 0
    return tm, tm_in, tq, tk


def kernel(x, c, ctx, c_ctx, w_ada, b_ada, g_mix, g_ffn, w_in, qg_a, kg_a, qg_b, kg_b, sink_b, conv_w, w_branch,
           w_out, w_router, w_e_gate, w_e_up, w_e_down):
    b, n, d = x.shape
    l_ctx = ctx.shape[1]
    depth = w_ada.shape[0]
    tm, tm_in, tq, tk = _tile_sizes(n, l_ctx)

    rows = -(-(b + 1) // 8) * 8
    s = jnp.concatenate([c, c_ctx[None, :], jnp.zeros((rows - b - 1, d), F32)], axis=0)
    mod_all = _ada(s, w_ada, b_ada)

    cs, sn = _rope_tables(n)
    cs_c = jnp.ones((l_ctx, LANES), F32)
    sn_c = jnp.zeros((l_ctx, LANES), F32)
    log2e = 1.4426950408889634
    scale = HEAD_DIM ** -0.5 * log2e

    w_bf = w_in.astype(BF16)
    wg, wu, wd = w_e_gate.astype(BF16), w_e_up.astype(BF16), w_e_down.astype(BF16)

    xc = ctx
    for l in range(depth):
        last = l == depth - 1
        mod = mod_all[l]
        sh1, sc1, gt1, sh2, sc2, gt2 = [m[:, None, :] for m in jnp.split(mod[:b], 6, axis=-1)]
        shc1, scc1, gtc1, shc2, scc2, gtc2 = [jnp.broadcast_to(m[None, None, :], (b, 1, d))
                                              for m in jnp.split(mod[b], 6)]
        gq =(jnp.concatenate([jnp.tile(qg_a[l], N_HEADS), jnp.tile(qg_b[l], N_HEADS)]) * scale)[None, :]
        gk = jnp.concatenate([jnp.tile(kg_a[l], N_KV), jnp.tile(kg_b[l], N_KV)])[None, :]
        gm = g_mix[l][None, :]
        gf = g_ffn[l][None, :]
        wb_bf = w_branch[l].astype(BF16)
        wo_bf = w_out[l].astype(BF16)
        wr_t = jnp.transpose(w_router[l])
        sink =sink_b[l].reshape(N_KV, GROUP) * log2e

        def sink_row(t):
            return jnp.repeat(sink, t, axis=1)[:, None, :]

        qa_c, qb_c, kta_c, va_c, ktb_c, vb_c, bg_c, u_c, gts_c = _inproj(
            xc, shc1, scc1, gm, w_bf, gq, gk, cs_c, sn_c, min(tm, l_ctx), l)
        qa, qb, kta, va, ktb, vb, bg, u, gts = _inproj(x, sh1, sc1, gm, w_bf, gq, gk, cs, sn, tm_in, l)

        k_all = jnp.concatenate([kta, kta_c], axis=1)
        vt_all = jnp.concatenate([va, va_c], axis=3)
        o_a = _attention(qa, k_all, vt_all, None, None, None, "global", tq, tk)
        o_b = _attention(qb, ktb, vb, ktb_c, vb_c, sink_row(tq), "window", tq, tk)
        x1, h2, aff = _merge(o_a, o_b, bg, u, gts, x, gt1, conv_w[l], wb_bf, wo_bf, gf, sh2, sc2, wr_t, tm_in)
        x = _moe(h2, aff, x1, gt2, wg, wu, wd, l)

        if not last:
            tqc = min(tq, l_ctx)
            oc_a = _attention(qa_c, None, None, kta_c, va_c, None, "none", tqc, tk)
            oc_b = _attention(qb_c, None, None, ktb_c, vb_c, sink_row(tqc), "none", tqc, tk)
            xc1, h2c, affc = _merge(oc_a, oc_b, bg_c, u_c, gts_c, xc, gtc1, conv_w[l], wb_bf, wo_bf, gf,
                                    shc2, scc2, wr_t, min(tm, l_ctx))
            xc = _moe(h2c, affc, xc1, gtc2, wg, wu, wd, l)
    return x
```
